```python
import jax, jax.numpy as jnp
from jax import lax
import numpy as np

D_MODEL = 4096
BATCH = 4
SEQ = 4096
DEPTH = 1

HEAD_DIM = 128
N_HEADS_NA = D_MODEL // (2 * HEAD_DIM)
N_HEADS_DIL = D_MODEL // (2 * HEAD_DIM)
D_NA = N_HEADS_NA * HEAD_DIM
D_DIL = N_HEADS_DIL * HEAD_DIM
D_MIX = D_NA + D_DIL
GRID_W = 64
NA_ROWS = 8
NA_COLS = 16
DIL_PAIRS = ((128, 1), (512, 4), (2048, 16))
DIL_BLOCK = 64
D_FF = ((8 * D_MODEL // 3 + 255) // 256) * 256
CONV_W = 3
EPS = 1e-6
NEG_INF = -1e30

kernel_name = 'hybrid_natten_dilated_convffn_block'


def rms_norm(x, g):
    xf = x.astype(jnp.float32)
    y = xf * lax.rsqrt(jnp.mean(xf * xf, axis=-1, keepdims=True) + EPS)
    return (y * g.astype(jnp.float32)).astype(x.dtype)


def split_heads(t, n):
    b, s, _ = t.shape
    return t.reshape(b, s, n, HEAD_DIM).transpose(0, 2, 1, 3)


def merge_heads(t):
    b, h, s, d = t.shape
    return t.transpose(0, 2, 1, 3).reshape(b, s, h * d)


def neighbourhood_attention(q, k, v, rel_bias):
    b, h, s, hd = q.shape
    rows = s // GRID_W
    kh = min(NA_ROWS, rows)
    qg = q.reshape(b, h, rows, GRID_W, hd)
    kg = k.reshape(b, h, rows, GRID_W, hd)
    vg = v.reshape(b, h, rows, GRID_W, hd)
    col = jnp.arange(GRID_W)
    c0 = jnp.clip(col - NA_COLS // 2, 0, GRID_W - NA_COLS)
    col_in = (col[None, :] >= c0[:, None]) & (col[None, :] < c0[:, None] + NA_COLS)
    dc_idx = jnp.clip(col[None, :] - col[:, None] + NA_COLS - 1, 0, 2 * NA_COLS - 2)
    col_bias = rel_bias[:, :, dc_idx].astype(jnp.float32)

    def one_row(r):
        r0 = jnp.clip(r - kh // 2, 0, rows - kh)
        q_r = lax.dynamic_index_in_dim(qg, r, axis=2, keepdims=False)
        k_r = lax.dynamic_slice_in_dim(kg, r0, kh, axis=2)
        v_r = lax.dynamic_slice_in_dim(vg, r0, kh, axis=2)
        dr_idx = r0 + jnp.arange(kh) - r + NA_ROWS - 1
        bias = jnp.take(col_bias, dr_idx, axis=1).transpose(0, 2, 1, 3)
        sc = jnp.einsum('bhqc,bhikc->bhqik', q_r, k_r).astype(jnp.float32) + bias[None]
        sc = jnp.where(col_in[:, None, :], sc, NEG_INF)
        p = jax.nn.softmax(sc.reshape(b, h, GRID_W, kh * GRID_W), axis=-1)
        p = p.reshape(b, h, GRID_W, kh, GRID_W).astype(v.dtype)
        return jnp.einsum('bhqik,bhikc->bhqc', p, v_r)

    out = lax.map(one_row, jnp.arange(rows))
    return out.transpose(1, 2, 0, 3, 4).reshape(b, h, s, hd)


def dilated_branch(q, k, v, window, dil, slopes):
    b, h, s, hd = q.shape
    half = window // (2 * dil)
    L = s // dil
    nb = -(-L // DIL_BLOCK)
    Lp = nb * DIL_BLOCK
    kb_len = DIL_BLOCK + 2 * half

    def to_res(t):
        return t.reshape(b, h, L, dil, hd).transpose(0, 1, 3, 2, 4)

    qr = jnp.pad(to_res(q), ((0, 0), (0, 0), (0, 0), (0, Lp - L), (0, 0)))
    kr = jnp.pad(to_res(k), ((0, 0), (0, 0), (0, 0), (half, Lp - L + half), (0, 0)))
    vr = jnp.pad(to_res(v), ((0, 0), (0, 0), (0, 0), (half, Lp - L + half), (0, 0)))
    qb = qr.reshape(b, h, dil, nb, DIL_BLOCK, hd)
    key_idx = (jnp.arange(nb) * DIL_BLOCK)[:, None] + jnp.arange(kb_len)[None, :]
    kb = kr[:, :, :, key_idx]
    vb = vr[:, :, :, key_idx]
    rel = jnp.arange(kb_len)[None, :] - half - jnp.arange(DIL_BLOCK)[:, None]
    kpos = (key_idx - half)[:, None, :]
    valid = (jnp.abs(rel) <= half)[None] & (kpos >= 0) & (kpos < L)
    dist = (dil * jnp.abs(rel)).astype(jnp.float32)
    sc = jnp.einsum('bhrnqc,bhrnkc->bhrnqk', qb, kb).astype(jnp.float32)
    sc = sc - slopes[None, :, None, None, None, None] * dist
    sc = jnp.where(valid, sc, NEG_INF)
    m = jnp.max(sc, axis=-1, keepdims=True)
    p = jnp.exp(sc - m)
    l = jnp.sum(p, axis=-1, keepdims=True)
    o = jnp.einsum('bhrnqk,bhrnkc->bhrnqc', p.astype(v.dtype), vb).astype(jnp.float32) / l
    lse = (m + jnp.log(l))[..., 0]
    o = o.reshape(b, h, dil, Lp, hd)[:, :, :, :L].transpose(0, 1, 3, 2, 4).reshape(b, h, s, hd)
    lse = lse.reshape(b, h, dil, Lp)[:, :, :, :L].transpose(0, 1, 3, 2).reshape(b, h, s)
    return o, lse


def dilated_attention(q, k, v, slopes):
    outs, lses = [], []
    for window, dil in DIL_PAIRS:
        o, lse = dilated_branch(q, k, v, window, dil, slopes)
        outs.append(o)
        lses.append(lse)
    w = jax.nn.softmax(jnp.stack(lses, axis=0), axis=0)
    out = jnp.einsum('nbhs,nbhsc->bhsc', w, jnp.stack(outs, axis=0))
    return out.astype(q.dtype)


def setup_inputs(seed: int = 0) -> dict:
    key = jax.random.key(seed)
    ks = jax.random.split(key, 16)
    f32 = jnp.float32
    nrm = lambda k, shape, sc: jax.random.normal(k, shape, f32) * sc
    x = jax.random.normal(ks[0], (BATCH, SEQ, D_MODEL), f32)
    norm1_g = 1.0 + nrm(ks[1], (DEPTH, D_MODEL), 0.02)
    w_in = nrm(ks[2], (DEPTH, D_MODEL, 3 * D_MIX), D_MODEL ** -0.5)
    qn_na = 1.0 + nrm(ks[3], (DEPTH, HEAD_DIM), 0.02)
    kn_na = 1.0 + nrm(ks[4], (DEPTH, HEAD_DIM), 0.02)
    qn_dil = 1.0 + nrm(ks[5], (DEPTH, HEAD_DIM), 0.02)
    kn_dil = 1.0 + nrm(ks[6], (DEPTH, HEAD_DIM), 0.02)
    rel_bias = nrm(ks[7], (DEPTH, N_HEADS_NA, 2 * NA_ROWS - 1, 2 * NA_COLS - 1), 0.5)
    out_norm_g = 1.0 + nrm(ks[8], (DEPTH, D_MIX), 0.02)
    w_out = nrm(ks[9], (DEPTH, D_MIX, D_MODEL), D_MIX ** -0.5)
    norm2_g = 1.0 + nrm(ks[10], (DEPTH, D_MODEL), 0.02)
    w_up = nrm(ks[11], (DEPTH, D_MODEL, 2 * D_FF), D_MODEL ** -0.5)
    conv_w = nrm(ks[12], (DEPTH, CONV_W, 2 * D_FF), CONV_W ** -0.5)
    conv_b = nrm(ks[13], (DEPTH, 2 * D_FF), 0.02)
    w_down = nrm(ks[14], (DEPTH, D_FF, D_MODEL), D_FF ** -0.5)
    return {'x': x, 'norm1_g': norm1_g, 'w_in': w_in, 'qn_na': qn_na, 'kn_na': kn_na,
            'qn_dil': qn_dil, 'kn_dil': kn_dil, 'rel_bias': rel_bias, 'out_norm_g': out_norm_g,
            'w_out': w_out, 'norm2_g': norm2_g, 'w_up': w_up, 'conv_w': conv_w,
            'conv_b': conv_b, 'w_down': w_down}


def reference(x, norm1_g, w_in, qn_na, kn_na, qn_dil, kn_dil, rel_bias, out_norm_g,
              w_out, norm2_g, w_up, conv_w, conv_b, w_down):
    scale = HEAD_DIM ** -0.5
    slopes = jnp.exp2(-8.0 * jnp.arange(1, N_HEADS_DIL + 1, dtype=jnp.float32) / N_HEADS_DIL)
    splits = [D_NA, 2 * D_NA, 3 * D_NA, 3 * D_NA + D_DIL, 3 * D_NA + 2 * D_DIL]
    for l in range(DEPTH):
        h = rms_norm(x, norm1_g[l])
        proj = h @ w_in[l]
        qa, ka, va, qd, kd, vd = jnp.split(proj, splits, axis=-1)
        qa = rms_norm(split_heads(qa, N_HEADS_NA), qn_na[l]) * scale
        ka = rms_norm(split_heads(ka, N_HEADS_NA), kn_na[l])
        va = split_heads(va, N_HEADS_NA)
        qd = rms_norm(split_heads(qd, N_HEADS_DIL), qn_dil[l]) * scale
        kd = rms_norm(split_heads(kd, N_HEADS_DIL), kn_dil[l])
        vd = split_heads(vd, N_HEADS_DIL)
        out_na = merge_heads(neighbourhood_attention(qa, ka, va, rel_bias[l]))
        out_dil = merge_heads(dilated_attention(qd, kd, vd, slopes))
        mix = jnp.concatenate([rms_norm(out_na, out_norm_g[l, :D_NA]),
                               rms_norm(out_dil, out_norm_g[l, D_NA:])], axis=-1)
        x = x + mix @ w_out[l]
        h2 = rms_norm(x, norm2_g[l])
        u = h2 @ w_up[l]
        u = lax.conv_general_dilated(
            u, conv_w[l].astype(u.dtype)[:, None, :], window_strides=(1,),
            padding=((CONV_W // 2, CONV_W // 2),), dimension_numbers=('NWC', 'WIO', 'NWC'),
            feature_group_count=u.shape[-1]) + conv_b[l].astype(u.dtype)
        gate, up = jnp.split(u, 2, axis=-1)
        x = x + (jax.nn.silu(gate) * up) @ w_down[l]
    return x
```

```python
import functools

import jax
import jax.numpy as jnp
import numpy as np
from jax import lax
from jax.experimental import pallas as pl
from jax.experimental.pallas import tpu as pltpu

D_MODEL = 4096
HEAD_DIM = 128
N_HEADS = 16
D_GRP = N_HEADS * HEAD_DIM
GRID_W = 64
NA_ROWS = 8
NA_COLS = 16
DIL_PAIRS = ((128, 1), (512, 4), (2048, 16))
DIL_BLOCK = 64
DIL_HALF = 64
D_FF = 11008
EPS = 1e-6
NEG_INF = -1e30

NA_QROWS = 4
NA_KROWS = NA_QROWS + NA_ROWS - 1
NA_QB = NA_QROWS * GRID_W
NA_KB = NA_KROWS * GRID_W

VMEM_LIMIT = 56 * 1024 * 1024

F32 = jnp.float32
BF16 = jnp.bfloat16


def _params(n_axes):
    return pltpu.CompilerParams(dimension_semantics=("arbitrary",) * n_axes,
                                vmem_limit_bytes=VMEM_LIMIT)


def _rmsnorm_kernel(x_ref, g_ref, o_ref):
    x = x_ref[...]
    ms = jnp.mean(x * x, axis=-1, keepdims=True)
    o_ref[...] = (x * lax.rsqrt(ms + EPS) * g_ref[...]).astype(o_ref.dtype)


def _rmsnorm(x, g, tr=512):
    t, d = x.shape
    return pl.pallas_call(
        _rmsnorm_kernel,
        grid=(t // tr,),
        in_specs=[pl.BlockSpec((tr, d), lambda i: (i, 0)),
                  pl.BlockSpec((1, d), lambda i: (0, 0))],
        out_specs=pl.BlockSpec((tr, d), lambda i: (i, 0)),
        out_shape=jax.ShapeDtypeStruct((t, d), BF16),
        compiler_params=_params(1),
        name="rmsnorm",
    )(x, g.reshape(1, d))


def _group_norm_kernel(a_ref, b_ref, g_ref, o_ref):
    for idx, ref in enumerate((a_ref, b_ref)):
        x = ref[...]
        ms = jnp.mean(x * x, axis=-1, keepdims=True)
        lo = idx * D_GRP
        o_ref[:, lo:lo + D_GRP] = (x * lax.rsqrt(ms + EPS) * g_ref[:, lo:lo + D_GRP]).astype(o_ref.dtype)


def _group_norm(a, b, g, tr=512):
    t = a.shape[0]
    return pl.pallas_call(
        _group_norm_kernel,
        grid=(t // tr,),
        in_specs=[pl.BlockSpec((tr, D_GRP), lambda i: (i, 0)),
                  pl.BlockSpec((tr, D_GRP), lambda i: (i, 0)),
                  pl.BlockSpec((1, 2 * D_GRP), lambda i: (0, 0))],
        out_specs=pl.BlockSpec((tr, 2 * D_GRP), lambda i: (i, 0)),
        out_shape=jax.ShapeDtypeStruct((t, 2 * D_GRP), BF16),
        compiler_params=_params(1),
        name="group_norm",
    )(a, b, g.reshape(1, 2 * D_GRP))


def _qkv_kernel(h_ref, w_ref, g_ref, o_ref, *, tn):
    acc = jnp.dot(h_ref[...], w_ref[...], preferred_element_type=F32)
    grp = (pl.program_id(1) * tn) // D_GRP
    is_v = (grp % 3) == 2

    @pl.when(is_v)
    def _():
        o_ref[...] = acc

    @pl.when(jnp.logical_not(is_v))
    def _():
        for c in range(tn // HEAD_DIM):
            sl = slice(c * HEAD_DIM, (c + 1) * HEAD_DIM)
            blk = acc[:, sl]
            ms = jnp.mean(blk * blk, axis=-1, keepdims=True)
            o_ref[:, sl] = blk * lax.rsqrt(ms + EPS) * g_ref[:, sl]


def _qkv_proj(h, w, gcol, tm=1024, tn=1024):
    t, d = h.shape
    n = w.shape[1]
    return pl.pallas_call(
        functools.partial(_qkv_kernel, tn=tn),
        grid=(t // tm, n // tn),
        in_specs=[pl.BlockSpec((tm, d), lambda i, j: (i, 0)),
                  pl.BlockSpec((d, tn), lambda i, j: (0, j)),
                  pl.BlockSpec((1, tn), lambda i, j: (0, j))],
        out_specs=pl.BlockSpec((tm, tn), lambda i, j: (i, j)),
        out_shape=jax.ShapeDtypeStruct((t, n), F32),
        compiler_params=_params(2),
        name="qkv_proj",
    )(h, w, gcol)


def _na_tables(rel_bias, rows):
    nblk = rows // NA_QROWS
    qr = np.arange(NA_QROWS)[:, None]
    kr = np.arange(NA_KROWS)[None, :]
    dr_idx = np.zeros((3, NA_QROWS, NA_KROWS), np.int32)
    row_ok = np.zeros((3, NA_QROWS, NA_KROWS), bool)
    for var, j in enumerate((0, 1, nblk - 1)):
        kstart = int(np.clip(NA_QROWS * j - NA_QROWS, 0, rows - NA_KROWS))
        r = NA_QROWS * j + qr
        r0 = np.clip(r - NA_ROWS // 2, 0, rows - NA_ROWS)
        ka = kstart + kr
        row_ok[var] = (ka >= r0) & (ka < r0 + NA_ROWS)
        dr_idx[var] = np.clip(ka - r + NA_ROWS - 1, 0, 2 * NA_ROWS - 2)
    col = np.arange(GRID_W)
    c0 = np.clip(col - NA_COLS // 2, 0, GRID_W - NA_COLS)
    col_ok = (col[None, :] >= c0[:, None]) & (col[None, :] < c0[:, None] + NA_COLS)
    dc_idx = np.clip(col[None, :] - col[:, None] + NA_COLS - 1, 0, 2 * NA_COLS - 2)

    col_bias = rel_bias[:, :, dc_idx]
    bias = col_bias[:, dr_idx]
    ok = row_ok[:, :, :, None, None] & col_ok[None, None, None, :, :]
    bias = jnp.where(ok[None], bias.astype(F32), NEG_INF)
    bias = bias.transpose(0, 1, 2, 4, 3, 5)
    return bias.reshape(rel_bias.shape[0], 3, NA_QB, NA_KB)


def _na_kernel(q_ref, k_ref, v_ref, tab_ref, o_ref, *, rows):
    nblk = rows // NA_QROWS

    def body(j, carry):
        krow = jnp.clip(NA_QROWS * j - NA_QROWS, 0, rows - NA_KROWS)
        var = jnp.where(j == 0, 0, jnp.where(j == nblk - 1, 2, 1))
        qs = pl.multiple_of(j * NA_QB, NA_QB)
        ks = pl.multiple_of(krow * GRID_W, GRID_W)
        q = q_ref[pl.ds(qs, NA_QB), :].astype(BF16)
        k = k_ref[pl.ds(ks, NA_KB), :].astype(BF16)
        v = v_ref[pl.ds(ks, NA_KB), :].astype(BF16)
        s = lax.dot_general(q, k, (((1,), (1,)), ((), ())), preferred_element_type=F32)
        s = s + tab_ref[0, var]
        m = jnp.max(s, axis=-1, keepdims=True)
        p = jnp.exp(s - m)
        l = jnp.sum(p, axis=-1, keepdims=True)
        o = jnp.dot(p.astype(BF16), v, preferred_element_type=F32)
        o_ref[pl.ds(qs, NA_QB), :] = o / l
        return carry

    lax.fori_loop(0, nblk, body, 0)


def _na_attention(proj, tab, batch, seq):
    t = proj.shape[0]
    rows = seq // GRID_W
    qkv_spec = lambda off: pl.BlockSpec((seq, HEAD_DIM), lambda h, b: (b, off + h))
    return pl.pallas_call(
        functools.partial(_na_kernel, rows=rows),
        grid=(N_HEADS, batch),
        in_specs=[qkv_spec(0), qkv_spec(N_HEADS), qkv_spec(2 * N_HEADS),
                  pl.BlockSpec((1, 3, NA_QB, NA_KB), lambda h, b: (h, 0, 0, 0))],
        out_specs=pl.BlockSpec((seq, HEAD_DIM), lambda h, b: (b, h)),
        out_shape=jax.ShapeDtypeStruct((t, D_GRP), F32),
        compiler_params=_params(2),
        name="na_attention",
    )(proj, proj, proj, tab)


def _dil_tables():
    kb = DIL_BLOCK + 2 * DIL_HALF
    slopes = jnp.exp2(-8.0 * jnp.arange(1, N_HEADS + 1, dtype=F32) / N_HEADS)
    i = np.arange(DIL_BLOCK)[:, None]
    j = np.arange(kb)[None, :]
    rel = j - DIL_HALF - i
    inwin = np.abs(rel) <= DIL_HALF
    ok = np.stack([inwin & (j >= DIL_HALF), inwin, inwin & (j < DIL_BLOCK + DIL_HALF)])
    tabs = []
    for _, dil in DIL_PAIRS:
        dist = jnp.asarray((dil * np.abs(rel)).astype(np.float32))
        pen = -(slopes[:, None, None] * dist[None])
        tabs.append(jnp.where(ok[None], pen[:, None], NEG_INF))
    return jnp.stack(tabs, axis=1)


def _dil_kernel(q_ref, k_ref, v_ref, tab_ref, o_ref,
                qp, kp, vp, op, lp, on0, on1, on2, ln0, ln1, ln2, *, seq):
    o_nat = (on0, on1, on2)
    l_nat = (ln0, ln1, ln2)
    kb = DIL_BLOCK + 2 * DIL_HALF
    for br, (_, dil) in enumerate(DIL_PAIRS):
        sub = seq // dil
        nb = sub // DIL_BLOCK
        subp = sub + 2 * DIL_HALF
        for r in range(dil):
            kbase = r * subp
            zpad = jnp.zeros((DIL_HALF, HEAD_DIM), BF16)
            kp[kbase:kbase + DIL_HALF, :] = zpad
            vp[kbase:kbase + DIL_HALF, :] = zpad
            kp[kbase + DIL_HALF + sub:kbase + subp, :] = zpad
            vp[kbase + DIL_HALF + sub:kbase + subp, :] = zpad
            if dil == 1:
                src = pl.ds(0, sub)
            else:
                src = pl.ds(r, sub, stride=dil)
            kp[kbase + DIL_HALF:kbase + DIL_HALF + sub, :] = k_ref[src, :].astype(BF16)
            vp[kbase + DIL_HALF:kbase + DIL_HALF + sub, :] = v_ref[src, :].astype(BF16)
            qp[r * sub:(r + 1) * sub, :] = q_ref[src, :].astype(BF16)

        def block(idx, carry, br=br, sub=sub, nb=nb, subp=subp):
            r = idx // nb
            n = idx % nb
            var = jnp.where(n == 0, 0, jnp.where(n == nb - 1, 2, 1))
            qs = pl.multiple_of(r * sub + n * DIL_BLOCK, DIL_BLOCK)
            ks = pl.multiple_of(r * subp + n * DIL_BLOCK, DIL_BLOCK)
            q = qp[pl.ds(qs, DIL_BLOCK), :]
            k = kp[pl.ds(ks, kb), :]
            v = vp[pl.ds(ks, kb), :]
            s = lax.dot_general(q, k, (((1,), (1,)), ((), ())), preferred_element_type=F32)
            s = s + tab_ref[0, br, var]
            m = jnp.max(s, axis=-1, keepdims=True)
            p = jnp.exp(s - m)
            l = jnp.sum(p, axis=-1, keepdims=True)
            o = jnp.dot(p.astype(BF16), v, preferred_element_type=F32)
            op[pl.ds(qs, DIL_BLOCK), :] = o / l
            lp[pl.ds(qs, DIL_BLOCK), :] = jnp.broadcast_to(m + jnp.log(l), (DIL_BLOCK, HEAD_DIM))
            return carry

        lax.fori_loop(0, dil * nb, block, 0)

        for r in range(dil):
            if dil == 1:
                dst = pl.ds(0, sub)
            else:
                dst = pl.ds(r, sub, stride=dil)
            o_nat[br][dst, :] = op[r * sub:(r + 1) * sub, :]
            l_nat[br][dst, :] = lp[r * sub:(r + 1) * sub, :]

    chunk = 256

    def merge(c, carry):
        sl = pl.ds(pl.multiple_of(c * chunk, chunk), chunk)
        l0, l1, l2 = ln0[sl, :], ln1[sl, :], ln2[sl, :]
        mx = jnp.maximum(jnp.maximum(l0, l1), l2)
        w0, w1, w2 = jnp.exp(l0 - mx), jnp.exp(l1 - mx), jnp.exp(l2 - mx)
        num = w0 * on0[sl, :] + w1 * on1[sl, :] + w2 * on2[sl, :]
        o_ref[sl, :] = num / (w0 + w1 + w2)
        return carry

    lax.fori_loop(0, seq // chunk, merge, 0)


def _dil_attention(proj, tab, batch, seq):
    t = proj.shape[0]
    kb = DIL_BLOCK + 2 * DIL_HALF
    max_dil = max(d for _, d in DIL_PAIRS)
    pad_rows = seq + 2 * DIL_HALF * max_dil
    qkv_spec = lambda off: pl.BlockSpec((seq, HEAD_DIM), lambda h, b: (b, off + h))
    nat = lambda: pltpu.VMEM((seq, HEAD_DIM), F32)
    return pl.pallas_call(
        functools.partial(_dil_kernel, seq=seq),
        grid=(N_HEADS, batch),
        in_specs=[qkv_spec(3 * N_HEADS), qkv_spec(4 * N_HEADS), qkv_spec(5 * N_HEADS),
                  pl.BlockSpec((1, len(DIL_PAIRS), 3, DIL_BLOCK, kb), lambda h, b: (h, 0, 0, 0, 0))],
        out_specs=pl.BlockSpec((seq, HEAD_DIM), lambda h, b: (b, h)),
        out_shape=jax.ShapeDtypeStruct((t, D_GRP), F32),
        scratch_shapes=[pltpu.VMEM((seq, HEAD_DIM), BF16),
                        pltpu.VMEM((pad_rows, HEAD_DIM), BF16),
                        pltpu.VMEM((pad_rows, HEAD_DIM), BF16),
                        nat(), nat(), nat(), nat(), nat(), nat(), nat(), nat()],
        compiler_params=_params(2),
        name="dil_attention",
    )(proj, proj, proj, tab)


def _mm_res_kernel(a_ref, w_ref, r_ref, o_ref):
    o_ref[...] = r_ref[...] + jnp.dot(a_ref[...], w_ref[...], preferred_element_type=F32)


def _mm_res(a, w, res, tm, tn, name):
    t, kdim = a.shape
    n = w.shape[1]
    return pl.pallas_call(
        _mm_res_kernel,
        grid=(t // tm, n // tn),
        in_specs=[pl.BlockSpec((tm, kdim), lambda i, j: (i, 0)),
                  pl.BlockSpec((kdim, tn), lambda i, j: (0, j)),
                  pl.BlockSpec((tm, tn), lambda i, j: (i, j))],
        out_specs=pl.BlockSpec((tm, tn), lambda i, j: (i, j)),
        out_shape=jax.ShapeDtypeStruct((t, n), F32),
        compiler_params=_params(2),
        name=name,
    )(a, w, res)


HALO = 8


def _ffn_up_kernel(h_ref, halo_ref, wg_ref, wu_ref, cwg_ref, cwu_ref, cbg_ref, cbu_ref, o_ref, *, tm):
    h = h_ref[...]
    halo = halo_ref[0]
    row = lax.broadcasted_iota(jnp.int32, (tm, 1), 0)

    def conv(w_ref, cw_ref, cb_ref):
        u = jnp.dot(h, w_ref[...], preferred_element_type=F32)
        uh = jnp.dot(halo, w_ref[...], preferred_element_type=F32)
        prev = jnp.where(row == 0, uh[HALO - 1:HALO, :], pltpu.roll(u, 1, axis=0))
        nxt = jnp.where(row == tm - 1, uh[HALO:HALO + 1, :], pltpu.roll(u, tm - 1, axis=0))
        return cw_ref[0:1, :] * prev + cw_ref[1:2, :] * u + cw_ref[2:3, :] * nxt + cb_ref[...]

    gate = conv(wg_ref, cwg_ref, cbg_ref)
    up = conv(wu_ref, cwu_ref, cbu_ref)
    o_ref[...] = (gate * jax.nn.sigmoid(gate) * up).astype(o_ref.dtype)


def _ffn_up(h2, halo, w_up, conv_w, conv_b, tm, tn=256):
    t, d = h2.shape
    nj = D_FF // tn
    cb = conv_b.reshape(1, 2 * D_FF)
    return pl.pallas_call(
        functools.partial(_ffn_up_kernel, tm=tm),
        grid=(t // tm, nj),
        in_specs=[pl.BlockSpec((tm, d), lambda i, j: (i, 0)),
                  pl.BlockSpec((1, 2 * HALO, d), lambda i, j: (i, 0, 0)),
                  pl.BlockSpec((d, tn), lambda i, j: (0, j)),
                  pl.BlockSpec((d, tn), lambda i, j: (0, j + nj)),
                  pl.BlockSpec((3, tn), lambda i, j: (0, j)),
                  pl.BlockSpec((3, tn), lambda i, j: (0, j + nj)),
                  pl.BlockSpec((1, tn), lambda i, j: (0, j)),
                  pl.BlockSpec((1, tn), lambda i, j: (0, j + nj))],
        out_specs=pl.BlockSpec((tm, tn), lambda i, j: (i, j)),
        out_shape=jax.ShapeDtypeStruct((t, D_FF), BF16),
        compiler_params=_params(2),
        name="ffn_up",
    )(h2, halo, w_up, w_up, conv_w, conv_w, cb, cb)


def _conv_halo(h2, tm, seq):
    t, d = h2.shape
    nt = t // tm
    hr = h2.reshape(nt, tm, d)
    zero = jnp.zeros((1, HALO, d), h2.dtype)
    prev = jnp.concatenate([zero, hr[:-1, tm - HALO:, :]], axis=0)
    nxt = jnp.concatenate([hr[1:, :HALO, :], zero], axis=0)
    start = (np.arange(nt) * tm) % seq == 0
    end = ((np.arange(nt) + 1) * tm) % seq == 0
    prev = jnp.where(jnp.asarray(start)[:, None, None], 0, prev)
    nxt = jnp.where(jnp.asarray(end)[:, None, None], 0, nxt)
    return jnp.concatenate([prev, nxt], axis=1)


def kernel(x, norm1_g, w_in, qn_na, kn_na, qn_dil, kn_dil, rel_bias, out_norm_g, w_out, norm2_g, w_up,
           conv_w, conv_b, w_down):
    batch, seq, d = x.shape
    t = batch * seq
    scale = HEAD_DIM ** -0.5
    ffn_tm = 1024
    xf = x.reshape(t, d)
    dil_tab = _dil_tables()
    ones = jnp.ones((D_GRP,), F32)
    for l in range(norm1_g.shape[0]):
        h = _rmsnorm(xf, norm1_g[l])
        gcol = jnp.concatenate([jnp.tile(qn_na[l] * scale, N_HEADS), jnp.tile(kn_na[l], N_HEADS), ones,
                                jnp.tile(qn_dil[l] * scale, N_HEADS), jnp.tile(kn_dil[l], N_HEADS), ones])
        proj = _qkv_proj(h, w_in[l].astype(BF16), gcol.reshape(1, -1))
        out_na = _na_attention(proj, _na_tables(rel_bias[l], seq // GRID_W), batch, seq)
        out_dil = _dil_attention(proj, dil_tab, batch, seq)
        mix = _group_norm(out_na, out_dil, out_norm_g[l])
        xf = _mm_res(mix, w_out[l].astype(BF16), xf, tm=1024, tn=1024, name="out_proj")
        h2 = _rmsnorm(xf, norm2_g[l])
        act = _ffn_up(h2, _conv_halo(h2, ffn_tm, seq), w_up[l].astype(BF16), conv_w[l], conv_b[l], tm=ffn_tm)
        xf = _mm_res(act, w_down[l].astype(BF16), xf, tm=512, tn=256, name="down_proj")
    return xf.reshape(batch, seq, d)
```

```python
import functools

import jax
import jax.numpy as jnp
import numpy as np
from jax import lax
from jax.experimental import pallas as pl
from jax.experimental.pallas import tpu as pltpu

D_MODEL = 4096
HEAD_DIM = 128
N_HEADS = 16
D_GRP = N_HEADS * HEAD_DIM
GRID_W = 64
NA_ROWS = 8
NA_COLS = 16
DIL_PAIRS = ((128, 1), (512, 4), (2048, 16))
DIL_BLOCK = 64
DIL_HALF = 64
D_FF = 11008
EPS = 1e-6
NEG_INF = -1e30

NA_QROWS = 4
NA_KROWS = NA_QROWS + NA_ROWS - 1
NA_QB = NA_QROWS * GRID_W
NA_KB = NA_KROWS * GRID_W

VMEM_LIMIT = 56 * 1024 * 1024

F32 = jnp.float32
BF16 = jnp.bfloat16


def _params(n_axes):
    return pltpu.CompilerParams(dimension_semantics=("arbitrary",) * n_axes,
                                vmem_limit_bytes=VMEM_LIMIT)


def _rmsnorm_kernel(x_ref, g_ref, o_ref):
    x = x_ref[...]
    ms = jnp.mean(x * x, axis=-1, keepdims=True)
    o_ref[...] = (x * lax.rsqrt(ms + EPS) * g_ref[...]).astype(o_ref.dtype)


def _rmsnorm(x, g, tr=512):
    t, d = x.shape
    return pl.pallas_call(
        _rmsnorm_kernel,
        grid=(t // tr,),
        in_specs=[pl.BlockSpec((tr, d), lambda i: (i, 0)),
                  pl.BlockSpec((1, d), lambda i: (0, 0))],
        out_specs=pl.BlockSpec((tr, d), lambda i: (i, 0)),
        out_shape=jax.ShapeDtypeStruct((t, d), BF16),
        compiler_params=_params(1),
        name="rmsnorm",
    )(x, g.reshape(1, d))


def _group_norm_kernel(a_ref, b_ref, g_ref, o_ref):
    for idx, ref in enumerate((a_ref, b_ref)):
        x = ref[...]
        ms = jnp.mean(x * x, axis=-1, keepdims=True)
        lo = idx * D_GRP
        o_ref[:, lo:lo + D_GRP] = (x * lax.rsqrt(ms + EPS) * g_ref[:, lo:lo + D_GRP]).astype(o_ref.dtype)


def _group_norm(a, b, g, tr=512):
    t = a.shape[0]
    return pl.pallas_call(
        _group_norm_kernel,
        grid=(t // tr,),
        in_specs=[pl.BlockSpec((tr, D_GRP), lambda i: (i, 0)),
                  pl.BlockSpec((tr, D_GRP), lambda i: (i, 0)),
                  pl.BlockSpec((1, 2 * D_GRP), lambda i: (0, 0))],
        out_specs=pl.BlockSpec((tr, 2 * D_GRP), lambda i: (i, 0)),
        out_shape=jax.ShapeDtypeStruct((t, 2 * D_GRP), BF16),
        compiler_params=_params(1),
        name="group_norm",
    )(a, b, g.reshape(1, 2 * D_GRP))


def _qkv_kernel(h_ref, w_ref, g_ref, o_ref, *, tn):
    acc = jnp.dot(h_ref[...], w_ref[...], preferred_element_type=F32)
    grp = (pl.program_id(1) * tn) // D_GRP
    is_v = (grp % 3) == 2

    @pl.when(is_v)
    def _():
        o_ref[...] = acc

    @pl.when(jnp.logical_not(is_v))
    def _():
        for c in range(tn // HEAD_DIM):
            sl = slice(c * HEAD_DIM, (c + 1) * HEAD_DIM)
            blk = acc[:, sl]
            ms = jnp.mean(blk * blk, axis=-1, keepdims=True)
            o_ref[:, sl] = blk * lax.rsqrt(ms + EPS) * g_ref[:, sl]


def _qkv_proj(h, w, gcol, tm=1024, tn=1024):
    t, d = h.shape
    n = w.shape[1]
    return pl.pallas_call(
        functools.partial(_qkv_kernel, tn=tn),
        grid=(t // tm, n // tn),
        in_specs=[pl.BlockSpec((tm, d), lambda i, j: (i, 0)),
                  pl.BlockSpec((d, tn), lambda i, j: (0, j)),
                  pl.BlockSpec((1, tn), lambda i, j: (0, j))],
        out_specs=pl.BlockSpec((tm, tn), lambda i, j: (i, j)),
        out_shape=jax.ShapeDtypeStruct((t, n), F32),
        compiler_params=_params(2),
        name="qkv_proj",
    )(h, w, gcol)


def _na_tables(rel_bias, rows):
    nblk = rows // NA_QROWS
    qr = np.arange(NA_QROWS)[:, None]
    kr = np.arange(NA_KROWS)[None, :]
    dr_idx = np.zeros((3, NA_QROWS, NA_KROWS), np.int32)
    row_ok = np.zeros((3, NA_QROWS, NA_KROWS), bool)
    for var, j in enumerate((0, 1, nblk - 1)):
        kstart = int(np.clip(NA_QROWS * j - NA_QROWS, 0, rows - NA_KROWS))
        r = NA_QROWS * j + qr
        r0 = np.clip(r - NA_ROWS // 2, 0, rows - NA_ROWS)
        ka = kstart + kr
        row_ok[var] = (ka >= r0) & (ka < r0 + NA_ROWS)
        dr_idx[var] = np.clip(ka - r + NA_ROWS - 1, 0, 2 * NA_ROWS - 2)
    col = np.arange(GRID_W)
    c0 = np.clip(col - NA_COLS // 2, 0, GRID_W - NA_COLS)
    col_ok = (col[None, :] >= c0[:, None]) & (col[None, :] < c0[:, None] + NA_COLS)
    dc_idx = np.clip(col[None, :] - col[:, None] + NA_COLS - 1, 0, 2 * NA_COLS - 2)

    col_bias = rel_bias[:, :, dc_idx]
    bias = col_bias[:, dr_idx]
    ok = row_ok[:, :, :, None, None] & col_ok[None, None, None, :, :]
    bias = jnp.where(ok[None], bias.astype(F32), NEG_INF)
    bias = bias.transpose(0, 1, 2, 4, 3, 5)
    return bias.reshape(rel_bias.shape[0], 3, NA_QB, NA_KB)


def _na_kernel(q_ref, k_ref, v_ref, tab_ref, o_ref, *, rows):
    nblk = rows // NA_QROWS

    def body(j, carry):
        krow = jnp.clip(NA_QROWS * j - NA_QROWS, 0, rows - NA_KROWS)
        var = jnp.where(j == 0, 0, jnp.where(j == nblk - 1, 2, 1))
        qs = pl.multiple_of(j * NA_QB, NA_QB)
        ks = pl.multiple_of(krow * GRID_W, GRID_W)
        q = q_ref[pl.ds(qs, NA_QB), :].astype(BF16)
        k = k_ref[pl.ds(ks, NA_KB), :].astype(BF16)
        v = v_ref[pl.ds(ks, NA_KB), :].astype(BF16)
        s = lax.dot_general(q, k, (((1,), (1,)), ((), ())), preferred_element_type=F32)
        s = s + tab_ref[0, var]
        m = jnp.max(s, axis=-1, keepdims=True)
        p = jnp.exp(s - m)
        l = jnp.sum(p, axis=-1, keepdims=True)
        o = jnp.dot(p.astype(BF16), v, preferred_element_type=F32)
        o_ref[pl.ds(qs, NA_QB), :] = o / l
        return carry

    lax.fori_loop(0, nblk, body, 0, unroll=2)


def _na_attention(proj, tab, batch, seq):
    t = proj.shape[0]
    rows = seq // GRID_W
    qkv_spec = lambda off: pl.BlockSpec((seq, HEAD_DIM), lambda h, b: (b, off + h))
    return pl.pallas_call(
        functools.partial(_na_kernel, rows=rows),
        grid=(N_HEADS, batch),
        in_specs=[qkv_spec(0), qkv_spec(N_HEADS), qkv_spec(2 * N_HEADS),
                  pl.BlockSpec((1, 3, NA_QB, NA_KB), lambda h, b: (h, 0, 0, 0))],
        out_specs=pl.BlockSpec((seq, HEAD_DIM), lambda h, b: (b, h)),
        out_shape=jax.ShapeDtypeStruct((t, D_GRP), F32),
        compiler_params=_params(2),
        name="na_attention",
    )(proj, proj, proj, tab)


DIL_QB = 2 * DIL_BLOCK
DIL_KB = DIL_QB + 2 * DIL_HALF
DIL_COPY = 512
DIL_MERGE = 64


def _dil_tables():
    slopes = jnp.exp2(-8.0 * jnp.arange(1, N_HEADS + 1, dtype=F32) / N_HEADS)
    i = np.arange(DIL_QB)[:, None]
    j = np.arange(DIL_KB)[None, :]
    rel = np.stack([j - off - i for off in (0, DIL_HALF, DIL_KB - DIL_QB)])
    ok = np.abs(rel) <= DIL_HALF
    tabs = []
    for _, dil in DIL_PAIRS:
        dist = jnp.asarray((dil * np.abs(rel)).astype(np.float32))
        pen = -(slopes[:, None, None, None] * dist[None])
        tabs.append(jnp.where(ok[None], pen, NEG_INF))
    return jnp.stack(tabs, axis=1)


def _dil_kernel(q_ref, k_ref, v_ref, tab_ref, o_ref, qp, kp, vp, op0, op1, op2, lp0, lp1, lp2, *, seq):
    o_res = (op0, op1, op2)
    l_res = (lp0, lp1, lp2)
    for br, (_, dil) in enumerate(DIL_PAIRS):
        sub = seq // dil
        nblk = sub // DIL_QB
        for r in range(dil):
            for c0 in range(0, sub, DIL_COPY):
                n = min(DIL_COPY, sub - c0)
                src = pl.ds(r + dil * c0, n, stride=dil) if dil > 1 else pl.ds(c0, n)
                dst = pl.ds(r * sub + c0, n)
                qp[dst, :] = q_ref[src, :].astype(BF16)
                kp[dst, :] = k_ref[src, :].astype(BF16)
                vp[dst, :] = v_ref[src, :].astype(BF16)

        def block(idx, carry, br=br, sub=sub, nblk=nblk):
            r = idx // nblk
            n = idx % nblk
            var = jnp.where(n == 0, 0, jnp.where(n == nblk - 1, 2, 1))
            qs = pl.multiple_of(r * sub + n * DIL_QB, DIL_QB)
            ks = pl.multiple_of(r * sub + jnp.clip(n * DIL_QB - DIL_HALF, 0, sub - DIL_KB), DIL_HALF)
            q = qp[pl.ds(qs, DIL_QB), :]
            k = kp[pl.ds(ks, DIL_KB), :]
            v = vp[pl.ds(ks, DIL_KB), :]
            s = lax.dot_general(q, k, (((1,), (1,)), ((), ())), preferred_element_type=F32)
            s = s + tab_ref[0, br, var]
            m = jnp.max(s, axis=-1, keepdims=True)
            p = jnp.exp(s - m)
            l = jnp.sum(p, axis=-1, keepdims=True)
            o = jnp.dot(p.astype(BF16), v, preferred_element_type=F32)
            o_res[br][pl.ds(qs, DIL_QB), :] = o / l
            l_res[br][pl.ds(qs, DIL_QB), :] = jnp.broadcast_to(m + jnp.log(l), (DIL_QB, HEAD_DIM))
            return carry

        lax.fori_loop(0, dil * nblk, block, 0, unroll=8)

    d2, d3 = DIL_PAIRS[1][1], DIL_PAIRS[2][1]
    sub2, sub3 = seq // d2, seq // d3
    step = d3 // d2
    nch = sub3 // DIL_MERGE

    def merge(idx, carry):
        r3 = idx // nch
        p0 = (idx % nch) * DIL_MERGE
        s3 = pl.ds(pl.multiple_of(r3 * sub3 + p0, DIL_MERGE), DIL_MERGE)
        s2 = pl.ds((r3 % d2) * sub2 + step * p0 + r3 // d2, DIL_MERGE, stride=step)
        s1 = pl.ds(d3 * p0 + r3, DIL_MERGE, stride=d3)
        l1, l2, l3 = lp0[s1, :], lp1[s2, :], lp2[s3, :]
        mx = jnp.maximum(jnp.maximum(l1, l2), l3)
        w1, w2, w3 = jnp.exp(l1 - mx), jnp.exp(l2 - mx), jnp.exp(l3 - mx)
        num = w1 * op0[s1, :] + w2 * op1[s2, :] + w3 * op2[s3, :]
        o_ref[s1, :] = num / (w1 + w2 + w3)
        return carry

    lax.fori_loop(0, d3 * nch, merge, 0, unroll=2)


def _dil_attention(proj, tab, batch, seq):
    t = proj.shape[0]
    dils = tuple(d for _, d in DIL_PAIRS)
    assert dils[0] == 1 and dils[2] % dils[1] == 0 and all(w // (2 * d) == DIL_HALF for w, d in DIL_PAIRS)
    assert all((seq // d) % DIL_QB == 0 and seq // d >= DIL_KB for d in dils)
    qkv_spec = lambda off: pl.BlockSpec((seq, HEAD_DIM), lambda h, b: (b, off + h))
    return pl.pallas_call(
        functools.partial(_dil_kernel, seq=seq),
        grid=(N_HEADS, batch),
        in_specs=[qkv_spec(3 * N_HEADS), qkv_spec(4 * N_HEADS), qkv_spec(5 * N_HEADS),
                  pl.BlockSpec((1, len(DIL_PAIRS), 3, DIL_QB, DIL_KB), lambda h, b: (h, 0, 0, 0, 0))],
        out_specs=pl.BlockSpec((seq, HEAD_DIM), lambda h, b: (b, h)),
        out_shape=jax.ShapeDtypeStruct((t, D_GRP), F32),
        scratch_shapes=[pltpu.VMEM((seq, HEAD_DIM), BF16)] * 3 + [pltpu.VMEM((seq, HEAD_DIM), F32)] * 6,
        compiler_params=_params(2),
        name="dil_attention",
    )(proj, proj, proj, tab)


def _mm_res_kernel(a_ref, w_ref, r_ref, o_ref):
    o_ref[...] = r_ref[...] + jnp.dot(a_ref[...], w_ref[...], preferred_element_type=F32)


def _mm_res(a, w, res, tm, tn, name):
    t, kdim = a.shape
    n = w.shape[1]
    return pl.pallas_call(
        _mm_res_kernel,
        grid=(t // tm, n // tn),
        in_specs=[pl.BlockSpec((tm, kdim), lambda i, j: (i, 0)),
                  pl.BlockSpec((kdim, tn), lambda i, j: (0, j)),
                  pl.BlockSpec((tm, tn), lambda i, j: (i, j))],
        out_specs=pl.BlockSpec((tm, tn), lambda i, j: (i, j)),
        out_shape=jax.ShapeDtypeStruct((t, n), F32),
        compiler_params=_params(2),
        name=name,
    )(a, w, res)


HALO = 16
FFN_ROWS = 64


def _ffn_up_kernel(h_ref, halo_ref, wg_ref, wu_ref, cwg_ref, cwu_ref, cbg_ref, cbu_ref, o_ref,
                   hx, ga, ua, gb, ub, *, tm, nj, last):
    s = pl.program_id(0)

    @pl.when(s == 0)
    def _():
        gb[...] = jnp.zeros_like(gb)
        ub[...] = jnp.zeros_like(ub)

    @pl.when(jnp.logical_and(s % nj == 0, s <= last))
    def _():
        hx[0:HALO, :] = halo_ref[0, 0:HALO, :]
        hx[HALO:HALO + tm, :] = h_ref[...]
        hx[HALO + tm:2 * HALO + tm, :] = halo_ref[0, HALO:2 * HALO, :]

    def phase(wr, rd):
        def conv(src, cw_ref, cb_ref, r0):
            prev = src[HALO - 1 + r0:HALO - 1 + r0 + FFN_ROWS, :]
            cur = src[HALO + r0:HALO + r0 + FFN_ROWS, :]
            nxt = src[HALO + 1 + r0:HALO + 1 + r0 + FFN_ROWS, :]
            return cw_ref[0:1, :] * prev + cw_ref[1:2, :] * cur + cw_ref[2:3, :] * nxt + cb_ref[...]

        d = hx.shape[1]
        nchunk = tm // FFN_ROWS
        kc = d // nchunk
        acc = [None, None]
        for c in range(nchunk):
            ksl = slice(c * kc, (c + 1) * kc)
            for i, w_ref in enumerate((wg_ref, wu_ref)):
                part = jnp.dot(hx[:, ksl], w_ref[ksl, :], preferred_element_type=F32)
                acc[i] = part if c == 0 else acc[i] + part
            r0 = c * FFN_ROWS
            gate = conv(rd[0], cwg_ref, cbg_ref, r0)
            up = conv(rd[1], cwu_ref, cbu_ref, r0)
            o_ref[r0:r0 + FFN_ROWS, :] = (gate * jax.nn.sigmoid(gate) * up).astype(o_ref.dtype)

        for i, dst in enumerate(wr):
            dst[...] = acc[i]

    @pl.when(s % 2 == 0)
    def _():
        phase((ga, ua), (gb, ub))

    @pl.when(s % 2 == 1)
    def _():
        phase((gb, ub), (ga, ua))


def _ffn_up(h2, halo, w_up, conv_w, conv_b, tm, tn=256):
    t, d = h2.shape
    nj = D_FF // tn
    last = (t // tm) * nj - 1
    cb = conv_b.reshape(1, 2 * D_FF)
    mm = lambda s: jnp.minimum(s, last)
    ep = lambda s: jnp.maximum(s - 1, 0)
    return pl.pallas_call(
        functools.partial(_ffn_up_kernel, tm=tm, nj=nj, last=last),
        grid=(last + 2,),
        in_specs=[pl.BlockSpec((tm, d), lambda s: (mm(s) // nj, 0)),
                  pl.BlockSpec((1, 2 * HALO, d), lambda s: (mm(s) // nj, 0, 0)),
                  pl.BlockSpec((d, tn), lambda s: (0, mm(s) % nj)),
                  pl.BlockSpec((d, tn), lambda s: (0, mm(s) % nj + nj)),
                  pl.BlockSpec((3, tn), lambda s: (0, ep(s) % nj)),
                  pl.BlockSpec((3, tn), lambda s: (0, ep(s) % nj + nj)),
                  pl.BlockSpec((1, tn), lambda s: (0, ep(s) % nj)),
                  pl.BlockSpec((1, tn), lambda s: (0, ep(s) % nj + nj))],
        out_specs=pl.BlockSpec((tm, tn), lambda s: (ep(s) // nj, ep(s) % nj)),
        out_shape=jax.ShapeDtypeStruct((t, D_FF), BF16),
        scratch_shapes=[pltpu.VMEM((tm + 2 * HALO, d), BF16)] + [pltpu.VMEM((tm + 2 * HALO, tn), F32)] * 4,
        compiler_params=_params(1),
        name="ffn_up",
    )(h2, halo, w_up, w_up, conv_w, conv_w, cb, cb)


def _conv_halo(h2, tm, seq):
    t, d = h2.shape
    nt = t // tm
    hr = h2.reshape(nt, tm, d)
    zero = jnp.zeros((1, HALO, d), h2.dtype)
    prev = jnp.concatenate([zero, hr[:-1, tm - HALO:, :]], axis=0)
    nxt = jnp.concatenate([hr[1:, :HALO, :], zero], axis=0)
    start = (np.arange(nt) * tm) % seq == 0
    end = ((np.arange(nt) + 1) * tm) % seq == 0
    prev = jnp.where(jnp.asarray(start)[:, None, None], 0, prev)
    nxt = jnp.where(jnp.asarray(end)[:, None, None], 0, nxt)
    return jnp.concatenate([prev, nxt], axis=1)


def kernel(x, norm1_g, w_in, qn_na, kn_na, qn_dil, kn_dil, rel_bias, out_norm_g, w_out, norm2_g, w_up,
           conv_w, conv_b, w_down):
    batch, seq, d = x.shape
    t = batch * seq
    scale = HEAD_DIM ** -0.5
    ffn_tm = 1024
    xf = x.reshape(t, d)
    dil_tab = _dil_tables()
    ones = jnp.ones((D_GRP,), F32)
    for l in range(norm1_g.shape[0]):
        h = _rmsnorm(xf, norm1_g[l])
        gcol = jnp.concatenate([jnp.tile(qn_na[l] * scale, N_HEADS), jnp.tile(kn_na[l], N_HEADS), ones,
                                jnp.tile(qn_dil[l] * scale, N_HEADS), jnp.tile(kn_dil[l], N_HEADS), ones])
        proj = _qkv_proj(h, w_in[l].astype(BF16), gcol.reshape(1, -1))
        out_na = _na_attention(proj, _na_tables(rel_bias[l], seq // GRID_W), batch, seq)
        out_dil = _dil_attention(proj, dil_tab, batch, seq)
        mix = _group_norm(out_na, out_dil, out_norm_g[l])
        xf = _mm_res(mix, w_out[l].astype(BF16), xf, tm=1024, tn=1024, name="out_proj")
        h2 = _rmsnorm(xf, norm2_g[l])
        act = _ffn_up(h2, _conv_halo(h2, ffn_tm, seq), w_up[l].astype(BF16), conv_w[l], conv_b[l], tm=ffn_tm)
        xf = _mm_res(act, w_down[l].astype(BF16), xf, tm=512, tn=256, name="down_proj")
    return xf.reshape(batch, seq, d)
```

```python
import functools

import jax
import jax.numpy as jnp
import numpy as np
from jax import lax
from jax.experimental import pallas as pl
from jax.experimental.pallas import tpu as pltpu

D_MODEL = 4096
HEAD_DIM = 128
N_HEADS = 16
D_GRP = N_HEADS * HEAD_DIM
GRID_W = 64
NA_ROWS = 8
NA_COLS = 16
DIL_PAIRS = ((128, 1), (512, 4), (2048, 16))
DIL_BLOCK = 64
DIL_HALF = 64
D_FF = 11008
EPS = 1e-6
NEG_INF = -1e30

NA_QROWS = 4
NA_KROWS = NA_QROWS + NA_ROWS - 1
NA_QB = NA_QROWS * GRID_W
NA_KB = NA_KROWS * GRID_W

VMEM_LIMIT = 56 * 1024 * 1024

F32 = jnp.float32
BF16 = jnp.bfloat16


def _params(n_axes):
    return pltpu.CompilerParams(dimension_semantics=("arbitrary",) * n_axes,
                                vmem_limit_bytes=VMEM_LIMIT)


def _rmsnorm_kernel(x_ref, g_ref, o_ref):
    x = x_ref[...]
    ms = jnp.mean(x * x, axis=-1, keepdims=True)
    o_ref[...] = (x * lax.rsqrt(ms + EPS) * g_ref[...]).astype(o_ref.dtype)


def _rmsnorm(x, g, tr=512):
    t, d = x.shape
    return pl.pallas_call(
        _rmsnorm_kernel,
        grid=(t // tr,),
        in_specs=[pl.BlockSpec((tr, d), lambda i: (i, 0)),
                  pl.BlockSpec((1, d), lambda i: (0, 0))],
        out_specs=pl.BlockSpec((tr, d), lambda i: (i, 0)),
        out_shape=jax.ShapeDtypeStruct((t, d), BF16),
        compiler_params=_params(1),
        name="rmsnorm",
    )(x, g.reshape(1, d))


def _group_norm_kernel(a_ref, b_ref, g_ref, o_ref):
    for idx, ref in enumerate((a_ref, b_ref)):
        x = ref[...]
        ms = jnp.mean(x * x, axis=-1, keepdims=True)
        lo = idx * D_GRP
        o_ref[:, lo:lo + D_GRP] = (x * lax.rsqrt(ms + EPS) * g_ref[:, lo:lo + D_GRP]).astype(o_ref.dtype)


def _group_norm(a, b, g, tr=512):
    t = a.shape[0]
    return pl.pallas_call(
        _group_norm_kernel,
        grid=(t // tr,),
        in_specs=[pl.BlockSpec((tr, D_GRP), lambda i: (i, 0)),
                  pl.BlockSpec((tr, D_GRP), lambda i: (i, 0)),
                  pl.BlockSpec((1, 2 * D_GRP), lambda i: (0, 0))],
        out_specs=pl.BlockSpec((tr, 2 * D_GRP), lambda i: (i, 0)),
        out_shape=jax.ShapeDtypeStruct((t, 2 * D_GRP), BF16),
        compiler_params=_params(1),
        name="group_norm",
    )(a, b, g.reshape(1, 2 * D_GRP))


def _qkv_kernel(h_ref, w_ref, g_ref, o_ref, *, tn):
    acc = jnp.dot(h_ref[...], w_ref[...], preferred_element_type=F32)
    grp = (pl.program_id(1) * tn) // D_GRP
    is_v = (grp % 3) == 2

    @pl.when(is_v)
    def _():
        o_ref[...] = acc

    @pl.when(jnp.logical_not(is_v))
    def _():
        for c in range(tn // HEAD_DIM):
            sl = slice(c * HEAD_DIM, (c + 1) * HEAD_DIM)
            blk = acc[:, sl]
            ms = jnp.mean(blk * blk, axis=-1, keepdims=True)
            o_ref[:, sl] = blk * lax.rsqrt(ms + EPS) * g_ref[:, sl]


def _qkv_proj(h, w, gcol, tm=1024, tn=1024):
    t, d = h.shape
    n = w.shape[1]
    return pl.pallas_call(
        functools.partial(_qkv_kernel, tn=tn),
        grid=(t // tm, n // tn),
        in_specs=[pl.BlockSpec((tm, d), lambda i, j: (i, 0)),
                  pl.BlockSpec((d, tn), lambda i, j: (0, j)),
                  pl.BlockSpec((1, tn), lambda i, j: (0, j))],
        out_specs=pl.BlockSpec((tm, tn), lambda i, j: (i, j)),
        out_shape=jax.ShapeDtypeStruct((t, n), F32),
        compiler_params=_params(2),
        name="qkv_proj",
    )(h, w, gcol)


def _na_tables(rel_bias, rows):
    nblk = rows // NA_QROWS
    qr = np.arange(NA_QROWS)[:, None]
    kr = np.arange(NA_KROWS)[None, :]
    dr_idx = np.zeros((3, NA_QROWS, NA_KROWS), np.int32)
    row_ok = np.zeros((3, NA_QROWS, NA_KROWS), bool)
    for var, j in enumerate((0, 1, nblk - 1)):
        kstart = int(np.clip(NA_QROWS * j - NA_QROWS, 0, rows - NA_KROWS))
        r = NA_QROWS * j + qr
        r0 = np.clip(r - NA_ROWS // 2, 0, rows - NA_ROWS)
        ka = kstart + kr
        row_ok[var] = (ka >= r0) & (ka < r0 + NA_ROWS)
        dr_idx[var] = np.clip(ka - r + NA_ROWS - 1, 0, 2 * NA_ROWS - 2)
    col = np.arange(GRID_W)
    c0 = np.clip(col - NA_COLS // 2, 0, GRID_W - NA_COLS)
    col_ok = (col[None, :] >= c0[:, None]) & (col[None, :] < c0[:, None] + NA_COLS)
    dc_idx = np.clip(col[None, :] - col[:, None] + NA_COLS - 1, 0, 2 * NA_COLS - 2)

    col_bias = rel_bias[:, :, dc_idx]
    bias = col_bias[:, dr_idx]
    ok = row_ok[:, :, :, None, None] & col_ok[None, None, None, :, :]
    bias = jnp.where(ok[None], bias.astype(F32), NEG_INF)
    bias = bias.transpose(0, 1, 2, 4, 3, 5)
    return bias.reshape(rel_bias.shape[0], 3, NA_QB, NA_KB)


def _na_kernel(q_ref, k_ref, v_ref, tab_ref, o_ref, *, rows):
    nblk = rows // NA_QROWS

    def body(j, carry):
        krow = jnp.clip(NA_QROWS * j - NA_QROWS, 0, rows - NA_KROWS)
        var = jnp.where(j == 0, 0, jnp.where(j == nblk - 1, 2, 1))
        qs = pl.multiple_of(j * NA_QB, NA_QB)
        ks = pl.multiple_of(krow * GRID_W, GRID_W)
        q = q_ref[pl.ds(qs, NA_QB), :].astype(BF16)
        k = k_ref[pl.ds(ks, NA_KB), :].astype(BF16)
        v = v_ref[pl.ds(ks, NA_KB), :].astype(BF16)
        s = lax.dot_general(q, k, (((1,), (1,)), ((), ())), preferred_element_type=F32)
        s = s + tab_ref[0, var]
        m = jnp.max(s, axis=-1, keepdims=True)
        p = jnp.exp(s - m)
        l = jnp.sum(p, axis=-1, keepdims=True)
        o = jnp.dot(p.astype(BF16), v, preferred_element_type=F32)
        o_ref[pl.ds(qs, NA_QB), :] = o / l
        return carry

    lax.fori_loop(0, nblk, body, 0, unroll=4)


def _na_attention(proj, tab, batch, seq):
    t = proj.shape[0]
    rows = seq // GRID_W
    qkv_spec = lambda off: pl.BlockSpec((seq, HEAD_DIM), lambda h, b: (b, off + h))
    return pl.pallas_call(
        functools.partial(_na_kernel, rows=rows),
        grid=(N_HEADS, batch),
        in_specs=[qkv_spec(0), qkv_spec(N_HEADS), qkv_spec(2 * N_HEADS),
                  pl.BlockSpec((1, 3, NA_QB, NA_KB), lambda h, b: (h, 0, 0, 0))],
        out_specs=pl.BlockSpec((seq, HEAD_DIM), lambda h, b: (b, h)),
        out_shape=jax.ShapeDtypeStruct((t, D_GRP), F32),
        compiler_params=_params(2),
        name="na_attention",
    )(proj, proj, proj, tab)


DIL_QB = 2 * DIL_BLOCK
DIL_KB = DIL_QB + 2 * DIL_HALF
DIL_COPY = 512
DIL_MERGE = 64


def _dil_tables():
    slopes = jnp.exp2(-8.0 * jnp.arange(1, N_HEADS + 1, dtype=F32) / N_HEADS)
    i = np.arange(DIL_QB)[:, None]
    j = np.arange(DIL_KB)[None, :]
    rel = np.stack([j - off - i for off in (0, DIL_HALF, DIL_KB - DIL_QB)])
    ok = np.abs(rel) <= DIL_HALF
    tabs = []
    for _, dil in DIL_PAIRS:
        dist = jnp.asarray((dil * np.abs(rel)).astype(np.float32))
        pen = -(slopes[:, None, None, None] * dist[None])
        tabs.append(jnp.where(ok[None], pen, NEG_INF))
    return jnp.stack(tabs, axis=1)


def _dil_kernel(q_ref, k_ref, v_ref, tab_ref, o_ref, qp, kp, vp, op0, op1, op2, lp0, lp1, lp2, *, seq):
    o_res = (op0, op1, op2)
    l_res = (lp0, lp1, lp2)
    for br, (_, dil) in enumerate(DIL_PAIRS):
        sub = seq // dil
        nblk = sub // DIL_QB
        for r in range(dil):
            for c0 in range(0, sub, DIL_COPY):
                n = min(DIL_COPY, sub - c0)
                src = pl.ds(r + dil * c0, n, stride=dil) if dil > 1 else pl.ds(c0, n)
                dst = pl.ds(r * sub + c0, n)
                qp[dst, :] = q_ref[src, :].astype(BF16)
                kp[dst, :] = k_ref[src, :].astype(BF16)
                vp[dst, :] = v_ref[src, :].astype(BF16)

        def block(idx, carry, br=br, sub=sub, nblk=nblk):
            r = idx // nblk
            n = idx % nblk
            var = jnp.where(n == 0, 0, jnp.where(n == nblk - 1, 2, 1))
            qs = pl.multiple_of(r * sub + n * DIL_QB, DIL_QB)
            ks = pl.multiple_of(r * sub + jnp.clip(n * DIL_QB - DIL_HALF, 0, sub - DIL_KB), DIL_HALF)
            q = qp[pl.ds(qs, DIL_QB), :]
            k = kp[pl.ds(ks, DIL_KB), :]
            v = vp[pl.ds(ks, DIL_KB), :]
            s = lax.dot_general(q, k, (((1,), (1,)), ((), ())), preferred_element_type=F32)
            s = s + tab_ref[0, br, var]
            m = jnp.max(s, axis=-1, keepdims=True)
            p = jnp.exp(s - m)
            l = jnp.sum(p, axis=-1, keepdims=True)
            o = jnp.dot(p.astype(BF16), v, preferred_element_type=F32)
            o_res[br][pl.ds(qs, DIL_QB), :] = o / l
            l_res[br][pl.ds(qs, DIL_QB), :] = jnp.broadcast_to(m + jnp.log(l), (DIL_QB, HEAD_DIM))
            return carry

        lax.fori_loop(0, dil * nblk, block, 0, unroll=16)

    d2, d3 = DIL_PAIRS[1][1], DIL_PAIRS[2][1]
    sub2, sub3 = seq // d2, seq // d3
    step = d3 // d2
    nch = sub3 // DIL_MERGE

    def merge(idx, carry):
        r3 = idx // nch
        p0 = (idx % nch) * DIL_MERGE
        s3 = pl.ds(pl.multiple_of(r3 * sub3 + p0, DIL_MERGE), DIL_MERGE)
        s2 = pl.ds((r3 % d2) * sub2 + step * p0 + r3 // d2, DIL_MERGE, stride=step)
        s1 = pl.ds(d3 * p0 + r3, DIL_MERGE, stride=d3)
        l1, l2, l3 = lp0[s1, :], lp1[s2, :], lp2[s3, :]
        mx = jnp.maximum(jnp.maximum(l1, l2), l3)
        w1, w2, w3 = jnp.exp(l1 - mx), jnp.exp(l2 - mx), jnp.exp(l3 - mx)
        num = w1 * op0[s1, :] + w2 * op1[s2, :] + w3 * op2[s3, :]
        o_ref[s1, :] = num / (w1 + w2 + w3)
        return carry

    lax.fori_loop(0, d3 * nch, merge, 0, unroll=4)


def _dil_attention(proj, tab, batch, seq):
    t = proj.shape[0]
    dils = tuple(d for _, d in DIL_PAIRS)
    assert dils[0] == 1 and dils[2] % dils[1] == 0 and all(w // (2 * d) == DIL_HALF for w, d in DIL_PAIRS)
    assert all((seq // d) % DIL_QB == 0 and seq // d >= DIL_KB for d in dils)
    qkv_spec = lambda off: pl.BlockSpec((seq, HEAD_DIM), lambda h, b: (b, off + h))
    return pl.pallas_call(
        functools.partial(_dil_kernel, seq=seq),
        grid=(N_HEADS, batch),
        in_specs=[qkv_spec(3 * N_HEADS), qkv_spec(4 * N_HEADS), qkv_spec(5 * N_HEADS),
                  pl.BlockSpec((1, len(DIL_PAIRS), 3, DIL_QB, DIL_KB), lambda h, b: (h, 0, 0, 0, 0))],
        out_specs=pl.BlockSpec((seq, HEAD_DIM), lambda h, b: (b, h)),
        out_shape=jax.ShapeDtypeStruct((t, D_GRP), F32),
        scratch_shapes=[pltpu.VMEM((seq, HEAD_DIM), BF16)] * 3 + [pltpu.VMEM((seq, HEAD_DIM), F32)] * 6,
        compiler_params=_params(2),
        name="dil_attention",
    )(proj, proj, proj, tab)


def _mm_res_kernel(a_ref, w_ref, r_ref, o_ref):
    o_ref[...] = r_ref[...] + jnp.dot(a_ref[...], w_ref[...], preferred_element_type=F32)


def _mm_res(a, w, res, tm, tn, name):
    t, kdim = a.shape
    n = w.shape[1]
    return pl.pallas_call(
        _mm_res_kernel,
        grid=(t // tm, n // tn),
        in_specs=[pl.BlockSpec((tm, kdim), lambda i, j: (i, 0)),
                  pl.BlockSpec((kdim, tn), lambda i, j: (0, j)),
                  pl.BlockSpec((tm, tn), lambda i, j: (i, j))],
        out_specs=pl.BlockSpec((tm, tn), lambda i, j: (i, j)),
        out_shape=jax.ShapeDtypeStruct((t, n), F32),
        compiler_params=_params(2),
        name=name,
    )(a, w, res)


HALO = 16
FFN_ROWS = 64


def _ffn_up_kernel(h_ref, halo_ref, wg_ref, wu_ref, cwg_ref, cwu_ref, cbg_ref, cbu_ref, o_ref,
                   hx, ga, ua, gb, ub, *, tm, nj, last):
    s = pl.program_id(0)

    @pl.when(s == 0)
    def _():
        gb[...] = jnp.zeros_like(gb)
        ub[...] = jnp.zeros_like(ub)

    @pl.when(jnp.logical_and(s % nj == 0, s <= last))
    def _():
        hx[0:HALO, :] = halo_ref[0, 0:HALO, :]
        hx[HALO:HALO + tm, :] = h_ref[...]
        hx[HALO + tm:2 * HALO + tm, :] = halo_ref[0, HALO:2 * HALO, :]

    def phase(wr, rd):
        def conv(src, cw_ref, cb_ref, r0):
            prev = src[HALO - 1 + r0:HALO - 1 + r0 + FFN_ROWS, :]
            cur = src[HALO + r0:HALO + r0 + FFN_ROWS, :]
            nxt = src[HALO + 1 + r0:HALO + 1 + r0 + FFN_ROWS, :]
            return cw_ref[0:1, :] * prev + cw_ref[1:2, :] * cur + cw_ref[2:3, :] * nxt + cb_ref[...]

        d = hx.shape[1]
        nchunk = tm // FFN_ROWS
        kc = d // nchunk
        acc = [None, None]
        for c in range(nchunk):
            ksl = slice(c * kc, (c + 1) * kc)
            for i, w_ref in enumerate((wg_ref, wu_ref)):
                part = jnp.dot(hx[:, ksl], w_ref[ksl, :], preferred_element_type=F32)
                acc[i] = part if c == 0 else acc[i] + part
            r0 = c * FFN_ROWS
            gate = conv(rd[0], cwg_ref, cbg_ref, r0)
            up = conv(rd[1], cwu_ref, cbu_ref, r0)
            o_ref[r0:r0 + FFN_ROWS, :] = (gate * jax.nn.sigmoid(gate) * up).astype(o_ref.dtype)

        for i, dst in enumerate(wr):
            dst[...] = acc[i]

    @pl.when(s % 2 == 0)
    def _():
        phase((ga, ua), (gb, ub))

    @pl.when(s % 2 == 1)
    def _():
        phase((gb, ub), (ga, ua))


def _ffn_up(h2, halo, w_up, conv_w, conv_b, tm, tn=256):
    t, d = h2.shape
    nj = D_FF // tn
    last = (t // tm) * nj - 1
    cb = conv_b.reshape(1, 2 * D_FF)
    mm = lambda s: jnp.minimum(s, last)
    ep = lambda s: jnp.maximum(s - 1, 0)
    return pl.pallas_call(
        functools.partial(_ffn_up_kernel, tm=tm, nj=nj, last=last),
        grid=(last + 2,),
        in_specs=[pl.BlockSpec((tm, d), lambda s: (mm(s) // nj, 0)),
                  pl.BlockSpec((1, 2 * HALO, d), lambda s: (mm(s) // nj, 0, 0)),
                  pl.BlockSpec((d, tn), lambda s: (0, mm(s) % nj)),
                  pl.BlockSpec((d, tn), lambda s: (0, mm(s) % nj + nj)),
                  pl.BlockSpec((3, tn), lambda s: (0, ep(s) % nj)),
                  pl.BlockSpec((3, tn), lambda s: (0, ep(s) % nj + nj)),
                  pl.BlockSpec((1, tn), lambda s: (0, ep(s) % nj)),
                  pl.BlockSpec((1, tn), lambda s: (0, ep(s) % nj + nj))],
        out_specs=pl.BlockSpec((tm, tn), lambda s: (ep(s) // nj, ep(s) % nj)),
        out_shape=jax.ShapeDtypeStruct((t, D_FF), BF16),
        scratch_shapes=[pltpu.VMEM((tm + 2 * HALO, d), BF16)] + [pltpu.VMEM((tm + 2 * HALO, tn), F32)] * 4,
        compiler_params=_params(1),
        name="ffn_up",
    )(h2, halo, w_up, w_up, conv_w, conv_w, cb, cb)


def _conv_halo(h2, tm, seq):
    t, d = h2.shape
    nt = t // tm
    hr = h2.reshape(nt, tm, d)
    zero = jnp.zeros((1, HALO, d), h2.dtype)
    prev = jnp.concatenate([zero, hr[:-1, tm - HALO:, :]], axis=0)
    nxt = jnp.concatenate([hr[1:, :HALO, :], zero], axis=0)
    start = (np.arange(nt) * tm) % seq == 0
    end = ((np.arange(nt) + 1) * tm) % seq == 0
    prev = jnp.where(jnp.asarray(start)[:, None, None], 0, prev)
    nxt = jnp.where(jnp.asarray(end)[:, None, None], 0, nxt)
    return jnp.concatenate([prev, nxt], axis=1)


def kernel(x, norm1_g, w_in, qn_na, kn_na, qn_dil, kn_dil, rel_bias, out_norm_g, w_out, norm2_g, w_up,
           conv_w, conv_b, w_down):
    batch, seq, d = x.shape
    t = batch * seq
    scale = HEAD_DIM ** -0.5
    ffn_tm = 1024
    xf = x.reshape(t, d)
    dil_tab = _dil_tables()
    ones = jnp.ones((D_GRP,), F32)
    for l in range(norm1_g.shape[0]):
        h = _rmsnorm(xf, norm1_g[l])
        gcol = jnp.concatenate([jnp.tile(qn_na[l] * scale, N_HEADS), jnp.tile(kn_na[l], N_HEADS), ones,
                                jnp.tile(qn_dil[l] * scale, N_HEADS), jnp.tile(kn_dil[l], N_HEADS), ones])
        proj = _qkv_proj(h, w_in[l].astype(BF16), gcol.reshape(1, -1))
        out_na = _na_attention(proj, _na_tables(rel_bias[l], seq // GRID_W), batch, seq)
        out_dil = _dil_attention(proj, dil_tab, batch, seq)
        mix = _group_norm(out_na, out_dil, out_norm_g[l])
        xf = _mm_res(mix, w_out[l].astype(BF16), xf, tm=1024, tn=1024, name="out_proj")
        h2 = _rmsnorm(xf, norm2_g[l])
        act = _ffn_up(h2, _conv_halo(h2, ffn_tm, seq), w_up[l].astype(BF16), conv_w[l], conv_b[l], tm=ffn_tm)
        xf = _mm_res(act, w_down[l].astype(BF16), xf, tm=512, tn=512, name="down_proj")
    return xf.reshape(batch, seq, d)
```

```python
import functools

import jax
import jax.numpy as jnp
import numpy as np
from jax import lax
from jax.experimental import pallas as pl
from jax.experimental.pallas import tpu as pltpu

D_MODEL = 4096
HEAD_DIM = 128
N_HEADS = 16
D_GRP = N_HEADS * HEAD_DIM
GRID_W = 64
NA_ROWS = 8
NA_COLS = 16
DIL_PAIRS = ((128, 1), (512, 4), (2048, 16))
DIL_BLOCK = 64
DIL_HALF = 64
D_FF = 11008
EPS = 1e-6
NEG_INF = -1e30

NA_QROWS = 4
NA_KROWS = NA_QROWS + NA_ROWS - 1
NA_QB = NA_QROWS * GRID_W
NA_KB = NA_KROWS * GRID_W

VMEM_LIMIT = 56 * 1024 * 1024

F32 = jnp.float32
BF16 = jnp.bfloat16


def _params(n_axes):
    return pltpu.CompilerParams(dimension_semantics=("arbitrary",) * n_axes,
                                vmem_limit_bytes=VMEM_LIMIT)


def _rmsnorm_kernel(x_ref, g_ref, o_ref):
    x = x_ref[...]
    ms = jnp.mean(x * x, axis=-1, keepdims=True)
    o_ref[...] = (x * lax.rsqrt(ms + EPS) * g_ref[...]).astype(o_ref.dtype)


def _rmsnorm(x, g, tr=512):
    t, d = x.shape
    return pl.pallas_call(
        _rmsnorm_kernel,
        grid=(t // tr,),
        in_specs=[pl.BlockSpec((tr, d), lambda i: (i, 0)),
                  pl.BlockSpec((1, d), lambda i: (0, 0))],
        out_specs=pl.BlockSpec((tr, d), lambda i: (i, 0)),
        out_shape=jax.ShapeDtypeStruct((t, d), BF16),
        compiler_params=_params(1),
        name="rmsnorm",
    )(x, g.reshape(1, d))


def _group_norm_kernel(a_ref, b_ref, g_ref, o_ref):
    for idx, ref in enumerate((a_ref, b_ref)):
        x = ref[...]
        ms = jnp.mean(x * x, axis=-1, keepdims=True)
        lo = idx * D_GRP
        o_ref[:, lo:lo + D_GRP] = (x * lax.rsqrt(ms + EPS) * g_ref[:, lo:lo + D_GRP]).astype(o_ref.dtype)


def _group_norm(a, b, g, tr=512):
    t = a.shape[0]
    return pl.pallas_call(
        _group_norm_kernel,
        grid=(t // tr,),
        in_specs=[pl.BlockSpec((tr, D_GRP), lambda i: (i, 0)),
                  pl.BlockSpec((tr, D_GRP), lambda i: (i, 0)),
                  pl.BlockSpec((1, 2 * D_GRP), lambda i: (0, 0))],
        out_specs=pl.BlockSpec((tr, 2 * D_GRP), lambda i: (i, 0)),
        out_shape=jax.ShapeDtypeStruct((t, 2 * D_GRP), BF16),
        compiler_params=_params(1),
        name="group_norm",
    )(a, b, g.reshape(1, 2 * D_GRP))


QKV_SLICES = 16
QKV_NORM_SLICES = 8
DIL2 = DIL_PAIRS[1][1]
DIL3 = DIL_PAIRS[2][1]


def _qkv_kernel(h_ref, w_ref, g_ref, ona_ref, od1_ref, od2_ref, od3_ref, hx, slab_a, slab_b, *, tm, tn, nj, last):
    s = pl.program_id(0)

    @pl.when(s == 0)
    def _():
        slab_b[...] = jnp.zeros_like(slab_b)

    @pl.when(jnp.logical_and(s % nj == 0, s <= last))
    def _():
        hx[0:tm, :] = h_ref[...]

    z0 = pl.multiple_of(jnp.minimum(s, 0), 16)
    je = jnp.maximum(s - 1, 0) % nj
    is_v = ((je * tn) // D_GRP) % 3 == 2
    nslab = tn // HEAD_DIM
    kc = h_ref.shape[1] // QKV_SLICES
    norm_rows = tm // QKV_NORM_SLICES
    perm_slices = QKV_SLICES - QKV_NORM_SLICES
    pieces2 = [(r, k) for r in range(DIL2) for k in range(nslab)]
    pieces3 = [(r, k) for r in range(DIL3) for k in range(nslab)]
    half = tn // 2

    def phase(wr, rd):
        def norm_rows_at(r0):
            for k in range(nslab):
                lanes = slice(k * HEAD_DIM, (k + 1) * HEAD_DIM)
                x = rd[k, r0:r0 + norm_rows, :]
                ms = jnp.mean(x * x, axis=-1, keepdims=True)
                y = x * jnp.where(is_v, 1.0, lax.rsqrt(ms + EPS)) * g_ref[:, lanes]
                rd[k, r0:r0 + norm_rows, :] = y
                yb = y.astype(BF16)
                ona_ref[r0:r0 + norm_rows, lanes] = yb
                od1_ref[r0:r0 + norm_rows, lanes] = yb
            return yb[0:16, :]

        def regroup(o_ref, dil, r, k):
            lanes = slice(k * HEAD_DIM, (k + 1) * HEAD_DIM)
            o_ref[0, r, :, lanes] = rd[k, pl.ds(r, tm // dil, stride=dil), :].astype(BF16)

        acc = [None, None]
        anchor = None
        for c in range(QKV_SLICES):
            ksl = slice(c * kc, (c + 1) * kc)
            for i in range(2):
                part = jnp.dot(hx[pl.ds(z0, tm), ksl], w_ref[ksl, i * half:(i + 1) * half],
                               preferred_element_type=F32)
                acc[i] = part if c == 0 else acc[i] + part
            if anchor is not None:
                hx[tm:tm + 16, 0:HEAD_DIM] = anchor
                anchor = None
            if c < QKV_NORM_SLICES:
                anchor = norm_rows_at(c * norm_rows)
            else:
                c2 = c - QKV_NORM_SLICES
                for r, k in pieces2[c2 * len(pieces2) // perm_slices:(c2 + 1) * len(pieces2) // perm_slices]:
                    regroup(od2_ref, DIL2, r, k)
                for r, k in pieces3[c2 * len(pieces3) // perm_slices:(c2 + 1) * len(pieces3) // perm_slices]:
                    regroup(od3_ref, DIL3, r, k)

        for i in range(2):
            for k2 in range(half // HEAD_DIM):
                wr[i * (half // HEAD_DIM) + k2, :, :] = acc[i][:, k2 * HEAD_DIM:(k2 + 1) * HEAD_DIM]

    @pl.when(s % 2 == 0)
    def _():
        phase(slab_a, slab_b)

    @pl.when(s % 2 == 1)
    def _():
        phase(slab_b, slab_a)


def _qkv_proj(h, w, gcol, batch, seq, tm=1024, tn=512):
    t, d = h.shape
    n = w.shape[1]
    nj = n // tn
    nna = 3 * D_GRP // tn
    tps = seq // tm
    last = (t // tm) * nj - 1
    assert seq % tm == 0 and D_GRP % tn == 0 and tn % (2 * HEAD_DIM) == 0 and tm % (QKV_NORM_SLICES * 16) == 0
    mm = lambda s: jnp.minimum(s, last)
    ep = lambda s: jnp.maximum(s - 1, 0)
    ie = lambda s: ep(s) // nj
    jna = lambda s: jnp.minimum(ep(s) % nj, nna)
    jdil = lambda s: jnp.maximum(ep(s) % nj - nna, 0)
    ndil = 3 * D_GRP
    return pl.pallas_call(
        functools.partial(_qkv_kernel, tm=tm, tn=tn, nj=nj, last=last),
        grid=(last + 2,),
        in_specs=[pl.BlockSpec((tm, d), lambda s: (mm(s) // nj, 0)),
                  pl.BlockSpec((d, tn), lambda s: (0, mm(s) % nj)),
                  pl.BlockSpec((1, tn), lambda s: (0, ep(s) % nj))],
        out_specs=[pl.BlockSpec((tm, tn), lambda s: (ie(s), jna(s))),
                   pl.BlockSpec((tm, tn), lambda s: (ie(s), jdil(s))),
                   pl.BlockSpec((1, DIL2, tm // DIL2, tn), lambda s: (ie(s) // tps, 0, ie(s) % tps, jdil(s))),
                   pl.BlockSpec((1, DIL3, tm // DIL3, tn), lambda s: (ie(s) // tps, 0, ie(s) % tps, jdil(s)))],
        out_shape=[jax.ShapeDtypeStruct((t, ndil + tn), BF16),
                   jax.ShapeDtypeStruct((t, ndil), BF16),
                   jax.ShapeDtypeStruct((batch, DIL2, seq // DIL2, ndil), BF16),
                   jax.ShapeDtypeStruct((batch, DIL3, seq // DIL3, ndil), BF16)],
        scratch_shapes=[pltpu.VMEM((tm + 16, d), BF16)] + [pltpu.VMEM((tn // HEAD_DIM, tm, HEAD_DIM), F32)] * 2,
        compiler_params=_params(1),
        name="qkv_proj",
    )(h, w, gcol)


def _na_tables(rel_bias, rows):
    nblk = rows // NA_QROWS
    qr = np.arange(NA_QROWS)[:, None]
    kr = np.arange(NA_KROWS)[None, :]
    dr_idx = np.zeros((3, NA_QROWS, NA_KROWS), np.int32)
    row_ok = np.zeros((3, NA_QROWS, NA_KROWS), bool)
    for var, j in enumerate((0, 1, nblk - 1)):
        kstart = int(np.clip(NA_QROWS * j - NA_QROWS, 0, rows - NA_KROWS))
        r = NA_QROWS * j + qr
        r0 = np.clip(r - NA_ROWS // 2, 0, rows - NA_ROWS)
        ka = kstart + kr
        row_ok[var] = (ka >= r0) & (ka < r0 + NA_ROWS)
        dr_idx[var] = np.clip(ka - r + NA_ROWS - 1, 0, 2 * NA_ROWS - 2)
    col = np.arange(GRID_W)
    c0 = np.clip(col - NA_COLS // 2, 0, GRID_W - NA_COLS)
    col_ok = (col[None, :] >= c0[:, None]) & (col[None, :] < c0[:, None] + NA_COLS)
    dc_idx = np.clip(col[None, :] - col[:, None] + NA_COLS - 1, 0, 2 * NA_COLS - 2)

    col_bias = rel_bias[:, :, dc_idx]
    bias = col_bias[:, dr_idx]
    ok = row_ok[:, :, :, None, None] & col_ok[None, None, None, :, :]
    bias = jnp.where(ok[None], bias.astype(F32), NEG_INF)
    bias = bias.transpose(0, 1, 2, 4, 3, 5)
    return bias.reshape(rel_bias.shape[0], 3, NA_QB, NA_KB)


def _na_kernel(q_ref, k_ref, v_ref, tab_ref, o_ref, *, rows):
    nblk = rows // NA_QROWS

    def body(j, carry):
        krow = jnp.clip(NA_QROWS * j - NA_QROWS, 0, rows - NA_KROWS)
        var = jnp.where(j == 0, 0, jnp.where(j == nblk - 1, 2, 1))
        qs = pl.multiple_of(j * NA_QB, NA_QB)
        ks = pl.multiple_of(krow * GRID_W, GRID_W)
        q = q_ref[pl.ds(qs, NA_QB), :]
        k = k_ref[pl.ds(ks, NA_KB), :]
        v = v_ref[pl.ds(ks, NA_KB), :]
        s = lax.dot_general(q, k, (((1,), (1,)), ((), ())), preferred_element_type=F32)
        s = s + tab_ref[0, var]
        m = jnp.max(s, axis=-1, keepdims=True)
        p = jnp.exp(s - m)
        l = jnp.sum(p, axis=-1, keepdims=True)
        o = jnp.dot(p.astype(BF16), v, preferred_element_type=F32)
        o_ref[pl.ds(qs, NA_QB), :] = o / l
        return carry

    lax.fori_loop(0, nblk, body, 0, unroll=4)


def _na_attention(proj, tab, batch, seq):
    t = proj.shape[0]
    rows = seq // GRID_W
    qkv_spec = lambda off: pl.BlockSpec((seq, HEAD_DIM), lambda h, b: (b, off + h))
    return pl.pallas_call(
        functools.partial(_na_kernel, rows=rows),
        grid=(N_HEADS, batch),
        in_specs=[qkv_spec(0), qkv_spec(N_HEADS), qkv_spec(2 * N_HEADS),
                  pl.BlockSpec((1, 3, NA_QB, NA_KB), lambda h, b: (h, 0, 0, 0))],
        out_specs=pl.BlockSpec((seq, HEAD_DIM), lambda h, b: (b, h)),
        out_shape=jax.ShapeDtypeStruct((t, D_GRP), F32),
        compiler_params=_params(2),
        name="na_attention",
    )(proj, proj, proj, tab)


DIL_QB = 2 * DIL_BLOCK
DIL_KB = DIL_QB + 2 * DIL_HALF


def _dil_tables():
    slopes = jnp.exp2(-8.0 * jnp.arange(1, N_HEADS + 1, dtype=F32) / N_HEADS)
    i = np.arange(DIL_QB)[:, None]
    j = np.arange(DIL_KB)[None, :]
    rel = np.stack([j - off - i for off in (0, DIL_HALF, DIL_KB - DIL_QB)])
    ok = np.abs(rel) <= DIL_HALF
    tabs = []
    for _, dil in DIL_PAIRS:
        dist = jnp.asarray((dil * np.abs(rel)).astype(np.float32))
        pen = -(slopes[:, None, None, None] * dist[None])
        tabs.append(jnp.where(ok[None], pen, NEG_INF))
    return jnp.stack(tabs, axis=1)


def _dil_kernel(q1, k1, v1, q2, k2, v2, q3, k3, v3, tab_ref, o_ref, op0, op1, lp0, lp1, *, seq):
    qkv = ((q1, k1, v1), (q2, k2, v2), (q3, k3, v3))
    o_res = (op0, op1)
    l_res = (lp0, lp1)
    d2, d3 = DIL_PAIRS[1][1], DIL_PAIRS[2][1]
    sub2 = seq // d2
    step = d3 // d2
    for br, (_, dil) in enumerate(DIL_PAIRS):
        sub = seq // dil
        nblk = sub // DIL_QB

        def block(idx, carry, br=br, sub=sub, nblk=nblk):
            q_ref, k_ref, v_ref = qkv[br]
            r = idx // nblk
            n = idx % nblk
            var = jnp.where(n == 0, 0, jnp.where(n == nblk - 1, 2, 1))
            q0 = pl.multiple_of(n * DIL_QB, DIL_QB)
            k0 = pl.multiple_of(jnp.clip(n * DIL_QB - DIL_HALF, 0, sub - DIL_KB), DIL_HALF)
            qs = pl.multiple_of(r * sub + n * DIL_QB, DIL_QB)
            q = q_ref[0, r, pl.ds(q0, DIL_QB), :]
            k = k_ref[0, r, pl.ds(k0, DIL_KB), :]
            v = v_ref[0, r, pl.ds(k0, DIL_KB), :]
            s = lax.dot_general(q, k, (((1,), (1,)), ((), ())), preferred_element_type=F32)
            s = s + tab_ref[0, br, var]
            m = jnp.max(s, axis=-1, keepdims=True)
            p = jnp.exp(s - m)
            l = jnp.sum(p, axis=-1, keepdims=True)
            o = jnp.dot(p.astype(BF16), v, preferred_element_type=F32)
            o3 = o / l
            l3 = jnp.broadcast_to(m + jnp.log(l), (DIL_QB, HEAD_DIM))
            if br < len(DIL_PAIRS) - 1:
                o_res[br][pl.ds(qs, DIL_QB), :] = o3
                l_res[br][pl.ds(qs, DIL_QB), :] = l3
            else:
                s2 = pl.ds((r % d2) * sub2 + step * q0 + r // d2, DIL_QB, stride=step)
                s1 = pl.ds(d3 * q0 + r, DIL_QB, stride=d3)
                l1, l2 = lp0[s1, :], lp1[s2, :]
                mx = jnp.maximum(jnp.maximum(l1, l2), l3)
                w1, w2, w3 = jnp.exp(l1 - mx), jnp.exp(l2 - mx), jnp.exp(l3 - mx)
                num = w1 * op0[s1, :] + w2 * op1[s2, :] + w3 * o3
                o_ref[s1, :] = num / (w1 + w2 + w3)
            return carry

        lax.fori_loop(0, dil * nblk, block, 0, unroll=16)


def _dil_attention(grouped, tab, batch, seq):
    dils = tuple(d for _, d in DIL_PAIRS)
    assert dils[0] == 1 and dils[2] % dils[1] == 0 and all(w // (2 * d) == DIL_HALF for w, d in DIL_PAIRS)
    assert all((seq // d) % DIL_QB == 0 and seq // d >= DIL_KB for d in dils)
    in_specs, args = [], []
    for arr, dil in zip(grouped, dils):
        for off in (0, N_HEADS, 2 * N_HEADS):
            in_specs.append(pl.BlockSpec((1, dil, seq // dil, HEAD_DIM), lambda h, b, off=off: (b, 0, 0, off + h)))
            args.append(arr)
    in_specs.append(pl.BlockSpec((1, len(DIL_PAIRS), 3, DIL_QB, DIL_KB), lambda h, b: (h, 0, 0, 0, 0)))
    return pl.pallas_call(
        functools.partial(_dil_kernel, seq=seq),
        grid=(N_HEADS, batch),
        in_specs=in_specs,
        out_specs=pl.BlockSpec((seq, HEAD_DIM), lambda h, b: (b, h)),
        out_shape=jax.ShapeDtypeStruct((batch * seq, D_GRP), F32),
        scratch_shapes=[pltpu.VMEM((seq, HEAD_DIM), F32)] * 4,
        compiler_params=_params(2),
        name="dil_attention",
    )(*args, tab)


def _mm_res_kernel(a_ref, w_ref, r_ref, o_ref):
    o_ref[...] = r_ref[...] + jnp.dot(a_ref[...], w_ref[...], preferred_element_type=F32)


def _mm_res(a, w, res, tm, tn, name):
    t, kdim = a.shape
    n = w.shape[1]
    return pl.pallas_call(
        _mm_res_kernel,
        grid=(t // tm, n // tn),
        in_specs=[pl.BlockSpec((tm, kdim), lambda i, j: (i, 0)),
                  pl.BlockSpec((kdim, tn), lambda i, j: (0, j)),
                  pl.BlockSpec((tm, tn), lambda i, j: (i, j))],
        out_specs=pl.BlockSpec((tm, tn), lambda i, j: (i, j)),
        out_shape=jax.ShapeDtypeStruct((t, n), F32),
        compiler_params=_params(2),
        name=name,
    )(a, w, res)


HALO = 16
FFN_ROWS = 64


def _ffn_up_kernel(h_ref, halo_ref, wg_ref, wu_ref, cwg_ref, cwu_ref, cbg_ref, cbu_ref, o_ref,
                   hx, ga, ua, gb, ub, *, tm, nj, last):
    s = pl.program_id(0)

    @pl.when(s == 0)
    def _():
        gb[...] = jnp.zeros_like(gb)
        ub[...] = jnp.zeros_like(ub)

    @pl.when(jnp.logical_and(s % nj == 0, s <= last))
    def _():
        hx[0:HALO, :] = halo_ref[0, 0:HALO, :]
        hx[HALO:HALO + tm, :] = h_ref[...]
        hx[HALO + tm:2 * HALO + tm, :] = halo_ref[0, HALO:2 * HALO, :]

    def phase(wr, rd):
        def conv(src, cw_ref, cb_ref, r0):
            prev = src[HALO - 1 + r0:HALO - 1 + r0 + FFN_ROWS, :]
            cur = src[HALO + r0:HALO + r0 + FFN_ROWS, :]
            nxt = src[HALO + 1 + r0:HALO + 1 + r0 + FFN_ROWS, :]
            return cw_ref[0:1, :] * prev + cw_ref[1:2, :] * cur + cw_ref[2:3, :] * nxt + cb_ref[...]

        d = hx.shape[1]
        nchunk = tm // FFN_ROWS
        kc = d // nchunk
        acc = [None, None]
        for c in range(nchunk):
            ksl = slice(c * kc, (c + 1) * kc)
            for i, w_ref in enumerate((wg_ref, wu_ref)):
                part = jnp.dot(hx[:, ksl], w_ref[ksl, :], preferred_element_type=F32)
                acc[i] = part if c == 0 else acc[i] + part
            r0 = c * FFN_ROWS
            gate = conv(rd[0], cwg_ref, cbg_ref, r0)
            up = conv(rd[1], cwu_ref, cbu_ref, r0)
            o_ref[r0:r0 + FFN_ROWS, :] = (gate * jax.nn.sigmoid(gate) * up).astype(o_ref.dtype)

        for i, dst in enumerate(wr):
            dst[...] = acc[i]

    @pl.when(s % 2 == 0)
    def _():
        phase((ga, ua), (gb, ub))

    @pl.when(s % 2 == 1)
    def _():
        phase((gb, ub), (ga, ua))


def _ffn_up(h2, halo, w_up, conv_w, conv_b, tm, tn=256):
    t, d = h2.shape
    nj = D_FF // tn
    last = (t // tm) * nj - 1
    cb = conv_b.reshape(1, 2 * D_FF)
    mm = lambda s: jnp.minimum(s, last)
    ep = lambda s: jnp.maximum(s - 1, 0)
    return pl.pallas_call(
        functools.partial(_ffn_up_kernel, tm=tm, nj=nj, last=last),
        grid=(last + 2,),
        in_specs=[pl.BlockSpec((tm, d), lambda s: (mm(s) // nj, 0)),
                  pl.BlockSpec((1, 2 * HALO, d), lambda s: (mm(s) // nj, 0, 0)),
                  pl.BlockSpec((d, tn), lambda s: (0, mm(s) % nj)),
                  pl.BlockSpec((d, tn), lambda s: (0, mm(s) % nj + nj)),
                  pl.BlockSpec((3, tn), lambda s: (0, ep(s) % nj)),
                  pl.BlockSpec((3, tn), lambda s: (0, ep(s) % nj + nj)),
                  pl.BlockSpec((1, tn), lambda s: (0, ep(s) % nj)),
                  pl.BlockSpec((1, tn), lambda s: (0, ep(s) % nj + nj))],
        out_specs=pl.BlockSpec((tm, tn), lambda s: (ep(s) // nj, ep(s) % nj)),
        out_shape=jax.ShapeDtypeStruct((t, D_FF), BF16),
        scratch_shapes=[pltpu.VMEM((tm + 2 * HALO, d), BF16)] + [pltpu.VMEM((tm + 2 * HALO, tn), F32)] * 4,
        compiler_params=_params(1),
        name="ffn_up",
    )(h2, halo, w_up, w_up, conv_w, conv_w, cb, cb)


def _conv_halo(h2, tm, seq):
    t, d = h2.shape
    nt = t // tm
    hr = h2.reshape(nt, tm, d)
    zero = jnp.zeros((1, HALO, d), h2.dtype)
    prev = jnp.concatenate([zero, hr[:-1, tm - HALO:, :]], axis=0)
    nxt = jnp.concatenate([hr[1:, :HALO, :], zero], axis=0)
    start = (np.arange(nt) * tm) % seq == 0
    end = ((np.arange(nt) + 1) * tm) % seq == 0
    prev = jnp.where(jnp.asarray(start)[:, None, None], 0, prev)
    nxt = jnp.where(jnp.asarray(end)[:, None, None], 0, nxt)
    return jnp.concatenate([prev, nxt], axis=1)


def kernel(x, norm1_g, w_in, qn_na, kn_na, qn_dil, kn_dil, rel_bias, out_norm_g, w_out, norm2_g, w_up,
           conv_w, conv_b, w_down):
    batch, seq, d = x.shape
    t = batch * seq
    scale = HEAD_DIM ** -0.5
    ffn_tm = 1024
    xf = x.reshape(t, d)
    dil_tab = _dil_tables()
    ones = jnp.ones((D_GRP,), F32)
    for l in range(norm1_g.shape[0]):
        h = _rmsnorm(xf, norm1_g[l])
        gcol = jnp.concatenate([jnp.tile(qn_na[l] * scale, N_HEADS), jnp.tile(kn_na[l], N_HEADS), ones,
                                jnp.tile(qn_dil[l] * scale, N_HEADS), jnp.tile(kn_dil[l], N_HEADS), ones])
        qkv_na, qkv_d1, qkv_d2, qkv_d3 = _qkv_proj(h, w_in[l].astype(BF16), gcol.reshape(1, -1), batch, seq)
        out_na = _na_attention(qkv_na, _na_tables(rel_bias[l], seq // GRID_W), batch, seq)
        out_dil = _dil_attention((qkv_d1.reshape(batch, 1, seq, -1), qkv_d2, qkv_d3), dil_tab, batch, seq)
        mix = _group_norm(out_na, out_dil, out_norm_g[l])
        xf = _mm_res(mix, w_out[l].astype(BF16), xf, tm=1024, tn=1024, name="out_proj")
        h2 = _rmsnorm(xf, norm2_g[l])
        act = _ffn_up(h2, _conv_halo(h2, ffn_tm, seq), w_up[l].astype(BF16), conv_w[l], conv_b[l], tm=ffn_tm)
        xf = _mm_res(act, w_down[l].astype(BF16), xf, tm=512, tn=512, name="down_proj")
    return xf.reshape(batch, seq, d)
```

```python
import functools

import jax
import jax.numpy as jnp
import numpy as np
from jax import lax
from jax.experimental import pallas as pl
from jax.experimental.pallas import tpu as pltpu

D_MODEL = 4096
HEAD_DIM = 128
N_HEADS = 16
D_GRP = N_HEADS * HEAD_DIM
GRID_W = 64
NA_ROWS = 8
NA_COLS = 16
DIL_PAIRS = ((128, 1), (512, 4), (2048, 16))
DIL_BLOCK = 64
DIL_HALF = 64
D_FF = 11008
EPS = 1e-6
NEG_INF = -1e30

NA_QROWS = 4
NA_KROWS = NA_QROWS + NA_ROWS - 1
NA_QB = NA_QROWS * GRID_W
NA_KB = NA_KROWS * GRID_W

VMEM_LIMIT = 56 * 1024 * 1024

F32 = jnp.float32
BF16 = jnp.bfloat16


def _params(n_axes):
    return pltpu.CompilerParams(dimension_semantics=("arbitrary",) * n_axes,
                                vmem_limit_bytes=VMEM_LIMIT)


def _rmsnorm_kernel(x_ref, g_ref, o_ref):
    x = x_ref[...]
    ms = jnp.mean(x * x, axis=-1, keepdims=True)
    o_ref[...] = (x * lax.rsqrt(ms + EPS) * g_ref[...]).astype(o_ref.dtype)


def _rmsnorm(x, g, tr=512):
    t, d = x.shape
    return pl.pallas_call(
        _rmsnorm_kernel,
        grid=(t // tr,),
        in_specs=[pl.BlockSpec((tr, d), lambda i: (i, 0)),
                  pl.BlockSpec((1, d), lambda i: (0, 0))],
        out_specs=pl.BlockSpec((tr, d), lambda i: (i, 0)),
        out_shape=jax.ShapeDtypeStruct((t, d), BF16),
        compiler_params=_params(1),
        name="rmsnorm",
    )(x, g.reshape(1, d))


def _group_norm_kernel(a_ref, b_ref, g_ref, o_ref):
    for idx, ref in enumerate((a_ref, b_ref)):
        x = ref[...]
        ms = jnp.mean(x * x, axis=-1, keepdims=True)
        lo = idx * D_GRP
        o_ref[:, lo:lo + D_GRP] = (x * lax.rsqrt(ms + EPS) * g_ref[:, lo:lo + D_GRP]).astype(o_ref.dtype)


def _group_norm(a, b, g, tr=512):
    t = a.shape[0]
    return pl.pallas_call(
        _group_norm_kernel,
        grid=(t // tr,),
        in_specs=[pl.BlockSpec((tr, D_GRP), lambda i: (i, 0)),
                  pl.BlockSpec((tr, D_GRP), lambda i: (i, 0)),
                  pl.BlockSpec((1, 2 * D_GRP), lambda i: (0, 0))],
        out_specs=pl.BlockSpec((tr, 2 * D_GRP), lambda i: (i, 0)),
        out_shape=jax.ShapeDtypeStruct((t, 2 * D_GRP), BF16),
        compiler_params=_params(1),
        name="group_norm",
    )(a, b, g.reshape(1, 2 * D_GRP))


DIL2 = DIL_PAIRS[1][1]
DIL3 = DIL_PAIRS[2][1]


QKV_SLICES = 16
CAST_BLOCKS = 128
CAST_BLOCK = ((128, 5504), (688, 512), (32, 4096))
CAST_PIECE = ((16, 5504), (688, 128), (16, 4096))


def _qkv_kernel(h_ref, w_ref, g_ref, c1_ref, c2_ref, c3_ref, nat_ref, od2_ref, od3_ref, b1_ref, b2_ref, b3_ref,
                slab, *, tm, tn, nna):
    j = pl.program_id(1)
    is_v = ((j * tn) // D_GRP) % 3 == 2
    nslab = tn // HEAD_DIM
    half = tn // 2
    kc = h_ref.shape[1] // QKV_SLICES

    @pl.when(pl.program_id(0) * pl.num_programs(1) + j < CAST_BLOCKS)
    def _():
        for src, dst, (br, bc), (pr, pc) in zip((c1_ref, c2_ref, c3_ref), (b1_ref, b2_ref, b3_ref), CAST_BLOCK,
                                                CAST_PIECE):
            for r in range(0, br, pr):
                for c in range(0, bc, pc):
                    dst[r:r + pr, c:c + pc] = src[r:r + pr, c:c + pc].astype(BF16)

    halves = [None, None]
    for c in range(QKV_SLICES):
        ksl = slice(c * kc, (c + 1) * kc)
        for i in range(2):
            part = jnp.dot(h_ref[:, ksl], w_ref[ksl, i * half:(i + 1) * half], preferred_element_type=F32)
            halves[i] = part if c == 0 else halves[i] + part

    def normed(k):
        lanes = slice(k * HEAD_DIM, (k + 1) * HEAD_DIM)
        per_half = half // HEAD_DIM
        x = halves[k // per_half][:, (k % per_half) * HEAD_DIM:(k % per_half + 1) * HEAD_DIM]
        ms = jnp.mean(x * x, axis=-1, keepdims=True)
        return x * jnp.where(is_v, 1.0, lax.rsqrt(ms + EPS)) * g_ref[:, lanes]

    @pl.when(j < nna)
    def _():
        for k in range(nslab):
            nat_ref[:, k * HEAD_DIM:(k + 1) * HEAD_DIM] = normed(k).astype(BF16)

    @pl.when(j >= nna)
    def _():
        for k in range(nslab):
            y = normed(k)
            slab[k, :, :] = y
            nat_ref[:, k * HEAD_DIM:(k + 1) * HEAD_DIM] = y.astype(BF16)
        for k in range(nslab):
            lanes = slice(k * HEAD_DIM, (k + 1) * HEAD_DIM)
            for o_ref, dil in ((od2_ref, DIL2), (od3_ref, DIL3)):
                for r in range(dil):
                    o_ref[0, r, :, lanes] = slab[k, pl.ds(r, tm // dil, stride=dil), :].astype(BF16)


def _qkv_proj(h, w, gcol, side_weights, batch, seq, tm=1024, tn=512):
    t, d = h.shape
    n = w.shape[1]
    nj = n // tn
    nna = 3 * D_GRP // tn
    tps = seq // tm
    assert seq % tm == 0 and D_GRP % tn == 0 and tm % (16 * DIL3) == 0 and tn % (2 * HEAD_DIM) == 0
    assert (t // tm) * nj >= CAST_BLOCKS
    jdil = lambda j: jnp.maximum(j - nna, 0)
    ndil = 3 * D_GRP
    cast_specs = []
    for arr, (br, bc) in zip(side_weights, CAST_BLOCK):
        ncol = arr.shape[1] // bc
        assert arr.shape[0] % br == 0 and arr.shape[1] % bc == 0 and (arr.shape[0] // br) * ncol == CAST_BLOCKS
        cast_specs.append(pl.BlockSpec(
            (br, bc), lambda i, j, ncol=ncol: (jnp.minimum(i * nj + j, CAST_BLOCKS - 1) // ncol,
                                               jnp.minimum(i * nj + j, CAST_BLOCKS - 1) % ncol)))
    return pl.pallas_call(
        functools.partial(_qkv_kernel, tm=tm, tn=tn, nna=nna),
        grid=(t // tm, nj),
        in_specs=[pl.BlockSpec((tm, d), lambda i, j: (i, 0)),
                  pl.BlockSpec((d, tn), lambda i, j: (0, j)),
                  pl.BlockSpec((1, tn), lambda i, j: (0, j))] + cast_specs,
        out_specs=[pl.BlockSpec((tm, tn), lambda i, j: (i, j)),
                   pl.BlockSpec((1, DIL2, tm // DIL2, tn), lambda i, j: (i // tps, 0, i % tps, jdil(j))),
                   pl.BlockSpec((1, DIL3, tm // DIL3, tn), lambda i, j: (i // tps, 0, i % tps, jdil(j)))]
        + cast_specs,
        out_shape=[jax.ShapeDtypeStruct((t, n), BF16),
                   jax.ShapeDtypeStruct((batch, DIL2, seq // DIL2, ndil), BF16),
                   jax.ShapeDtypeStruct((batch, DIL3, seq // DIL3, ndil), BF16)]
        + [jax.ShapeDtypeStruct(a.shape, BF16) for a in side_weights],
        scratch_shapes=[pltpu.VMEM((tn // HEAD_DIM, tm, HEAD_DIM), F32)],
        compiler_params=_params(2),
        name="qkv_proj",
    )(h, w, gcol, *side_weights)


def _na_tables(rel_bias, rows):
    nblk = rows // NA_QROWS
    qr = np.arange(NA_QROWS)[:, None]
    kr = np.arange(NA_KROWS)[None, :]
    dr_idx = np.zeros((3, NA_QROWS, NA_KROWS), np.int32)
    row_ok = np.zeros((3, NA_QROWS, NA_KROWS), bool)
    for var, j in enumerate((0, 1, nblk - 1)):
        kstart = int(np.clip(NA_QROWS * j - NA_QROWS, 0, rows - NA_KROWS))
        r = NA_QROWS * j + qr
        r0 = np.clip(r - NA_ROWS // 2, 0, rows - NA_ROWS)
        ka = kstart + kr
        row_ok[var] = (ka >= r0) & (ka < r0 + NA_ROWS)
        dr_idx[var] = np.clip(ka - r + NA_ROWS - 1, 0, 2 * NA_ROWS - 2)
    col = np.arange(GRID_W)
    c0 = np.clip(col - NA_COLS // 2, 0, GRID_W - NA_COLS)
    col_ok = (col[None, :] >= c0[:, None]) & (col[None, :] < c0[:, None] + NA_COLS)
    dc_idx = np.clip(col[None, :] - col[:, None] + NA_COLS - 1, 0, 2 * NA_COLS - 2)

    col_bias = rel_bias[:, :, dc_idx]
    bias = col_bias[:, dr_idx]
    ok = row_ok[:, :, :, None, None] & col_ok[None, None, None, :, :]
    bias = jnp.where(ok[None], bias.astype(F32), NEG_INF)
    bias = bias.transpose(0, 1, 2, 4, 3, 5)
    return bias.reshape(rel_bias.shape[0], 3, NA_QB, NA_KB)


def _na_kernel(q_ref, k_ref, v_ref, tab_ref, o_ref, *, rows):
    nblk = rows // NA_QROWS

    def body(j, carry):
        krow = jnp.clip(NA_QROWS * j - NA_QROWS, 0, rows - NA_KROWS)
        var = jnp.where(j == 0, 0, jnp.where(j == nblk - 1, 2, 1))
        qs = pl.multiple_of(j * NA_QB, NA_QB)
        ks = pl.multiple_of(krow * GRID_W, GRID_W)
        q = q_ref[pl.ds(qs, NA_QB), :]
        k = k_ref[pl.ds(ks, NA_KB), :]
        v = v_ref[pl.ds(ks, NA_KB), :]
        s = lax.dot_general(q, k, (((1,), (1,)), ((), ())), preferred_element_type=F32)
        s = s + tab_ref[0, var]
        m = jnp.max(s, axis=-1, keepdims=True)
        p = jnp.exp(s - m)
        l = jnp.sum(p, axis=-1, keepdims=True)
        o = jnp.dot(p.astype(BF16), v, preferred_element_type=F32)
        o_ref[pl.ds(qs, NA_QB), :] = o / l
        return carry

    lax.fori_loop(0, nblk, body, 0, unroll=4)


def _na_attention(proj, tab, batch, seq):
    t = proj.shape[0]
    rows = seq // GRID_W
    qkv_spec = lambda off: pl.BlockSpec((seq, HEAD_DIM), lambda h, b: (b, off + h))
    return pl.pallas_call(
        functools.partial(_na_kernel, rows=rows),
        grid=(N_HEADS, batch),
        in_specs=[qkv_spec(0), qkv_spec(N_HEADS), qkv_spec(2 * N_HEADS),
                  pl.BlockSpec((1, 3, NA_QB, NA_KB), lambda h, b: (h, 0, 0, 0))],
        out_specs=pl.BlockSpec((seq, HEAD_DIM), lambda h, b: (b, h)),
        out_shape=jax.ShapeDtypeStruct((t, D_GRP), F32),
        compiler_params=_params(2),
        name="na_attention",
    )(proj, proj, proj, tab)


DIL_QB = 2 * DIL_BLOCK
DIL_KB = DIL_QB + 2 * DIL_HALF


def _dil_tables():
    slopes = jnp.exp2(-8.0 * jnp.arange(1, N_HEADS + 1, dtype=F32) / N_HEADS)
    i = np.arange(DIL_QB)[:, None]
    j = np.arange(DIL_KB)[None, :]
    rel = np.stack([j - off - i for off in (0, DIL_HALF, DIL_KB - DIL_QB)])
    ok = np.abs(rel) <= DIL_HALF
    tabs = []
    for _, dil in DIL_PAIRS:
        dist = jnp.asarray((dil * np.abs(rel)).astype(np.float32))
        pen = -(slopes[:, None, None, None] * dist[None])
        tabs.append(jnp.where(ok[None], pen, NEG_INF))
    return jnp.stack(tabs, axis=1)


def _dil_kernel(q1, k1, v1, q2, k2, v2, q3, k3, v3, tab_ref, o_ref, op0, op1, lp0, lp1, *, seq):
    qkv = ((q1, k1, v1), (q2, k2, v2), (q3, k3, v3))
    o_res = (op0, op1)
    l_res = (lp0, lp1)
    d2, d3 = DIL_PAIRS[1][1], DIL_PAIRS[2][1]
    sub2 = seq // d2
    step = d3 // d2
    for br, (_, dil) in enumerate(DIL_PAIRS):
        sub = seq // dil
        nblk = sub // DIL_QB

        def block(idx, carry, br=br, sub=sub, nblk=nblk):
            q_ref, k_ref, v_ref = qkv[br]
            r = idx // nblk
            n = idx % nblk
            var = jnp.where(n == 0, 0, jnp.where(n == nblk - 1, 2, 1))
            q0 = pl.multiple_of(n * DIL_QB, DIL_QB)
            k0 = pl.multiple_of(jnp.clip(n * DIL_QB - DIL_HALF, 0, sub - DIL_KB), DIL_HALF)
            qs = pl.multiple_of(r * sub + n * DIL_QB, DIL_QB)
            q = q_ref[0, r, pl.ds(q0, DIL_QB), :]
            k = k_ref[0, r, pl.ds(k0, DIL_KB), :]
            v = v_ref[0, r, pl.ds(k0, DIL_KB), :]
            s = lax.dot_general(q, k, (((1,), (1,)), ((), ())), preferred_element_type=F32)
            s = s + tab_ref[0, br, var]
            m = jnp.max(s, axis=-1, keepdims=True)
            p = jnp.exp(s - m)
            l = jnp.sum(p, axis=-1, keepdims=True)
            o = jnp.dot(p.astype(BF16), v, preferred_element_type=F32)
            o3 = o / l
            l3 = jnp.broadcast_to(m + jnp.log(l), (DIL_QB, HEAD_DIM))
            if br < len(DIL_PAIRS) - 1:
                o_res[br][pl.ds(qs, DIL_QB), :] = o3
                l_res[br][pl.ds(qs, DIL_QB), :] = l3
            else:
                s2 = pl.ds((r % d2) * sub2 + step * q0 + r // d2, DIL_QB, stride=step)
                s1 = pl.ds(d3 * q0 + r, DIL_QB, stride=d3)
                l1, l2 = lp0[s1, :], lp1[s2, :]
                mx = jnp.maximum(jnp.maximum(l1, l2), l3)
                w1, w2, w3 = jnp.exp(l1 - mx), jnp.exp(l2 - mx), jnp.exp(l3 - mx)
                num = w1 * op0[s1, :] + w2 * op1[s2, :] + w3 * o3
                o_ref[s1, :] = num / (w1 + w2 + w3)
            return carry

        lax.fori_loop(0, dil * nblk, block, 0, unroll=16)


def _dil_attention(grouped, tab, batch, seq):
    dils = tuple(d for _, d in DIL_PAIRS)
    assert dils[0] == 1 and dils[2] % dils[1] == 0 and all(w // (2 * d) == DIL_HALF for w, d in DIL_PAIRS)
    assert all((seq // d) % DIL_QB == 0 and seq // d >= DIL_KB for d in dils)
    in_specs, args = [], []
    for (arr, base), dil in zip(grouped, dils):
        for off in (base, base + N_HEADS, base + 2 * N_HEADS):
            in_specs.append(pl.BlockSpec((1, dil, seq // dil, HEAD_DIM), lambda h, b, off=off: (b, 0, 0, off + h)))
            args.append(arr)
    in_specs.append(pl.BlockSpec((1, len(DIL_PAIRS), 3, DIL_QB, DIL_KB), lambda h, b: (h, 0, 0, 0, 0)))
    return pl.pallas_call(
        functools.partial(_dil_kernel, seq=seq),
        grid=(N_HEADS, batch),
        in_specs=in_specs,
        out_specs=pl.BlockSpec((seq, HEAD_DIM), lambda h, b: (b, h)),
        out_shape=jax.ShapeDtypeStruct((batch * seq, D_GRP), F32),
        scratch_shapes=[pltpu.VMEM((seq, HEAD_DIM), F32)] * 4,
        compiler_params=_params(2),
        name="dil_attention",
    )(*args, tab)


def _mm_res_kernel(a_ref, w_ref, r_ref, o_ref):
    o_ref[...] = r_ref[...] + jnp.dot(a_ref[...], w_ref[...], preferred_element_type=F32)


def _mm_res(a, w, res, tm, tn, name):
    t, kdim = a.shape
    n = w.shape[1]
    return pl.pallas_call(
        _mm_res_kernel,
        grid=(t // tm, n // tn),
        in_specs=[pl.BlockSpec((tm, kdim), lambda i, j: (i, 0)),
                  pl.BlockSpec((kdim, tn), lambda i, j: (0, j)),
                  pl.BlockSpec((tm, tn), lambda i, j: (i, j))],
        out_specs=pl.BlockSpec((tm, tn), lambda i, j: (i, j)),
        out_shape=jax.ShapeDtypeStruct((t, n), F32),
        compiler_params=_params(2),
        name=name,
    )(a, w, res)


HALO = 16
FFN_ROWS = 64


def _ffn_up_kernel(h_ref, halo_ref, wg_ref, wu_ref, cwg_ref, cwu_ref, cbg_ref, cbu_ref, o_ref,
                   hx, ga, ua, gb, ub, *, tm, nj, last):
    s = pl.program_id(0)

    @pl.when(s == 0)
    def _():
        gb[...] = jnp.zeros_like(gb)
        ub[...] = jnp.zeros_like(ub)

    @pl.when(jnp.logical_and(s % nj == 0, s <= last))
    def _():
        hx[0:HALO, :] = halo_ref[0, 0:HALO, :]
        hx[HALO:HALO + tm, :] = h_ref[...]
        hx[HALO + tm:2 * HALO + tm, :] = halo_ref[0, HALO:2 * HALO, :]

    def phase(wr, rd):
        def conv(src, cw_ref, cb_ref, r0):
            prev = src[HALO - 1 + r0:HALO - 1 + r0 + FFN_ROWS, :]
            cur = src[HALO + r0:HALO + r0 + FFN_ROWS, :]
            nxt = src[HALO + 1 + r0:HALO + 1 + r0 + FFN_ROWS, :]
            return cw_ref[0:1, :] * prev + cw_ref[1:2, :] * cur + cw_ref[2:3, :] * nxt + cb_ref[...]

        d = hx.shape[1]
        nchunk = tm // FFN_ROWS
        kc = d // nchunk
        acc = [None, None]
        for c in range(nchunk):
            ksl = slice(c * kc, (c + 1) * kc)
            for i, w_ref in enumerate((wg_ref, wu_ref)):
                part = jnp.dot(hx[:, ksl], w_ref[ksl, :], preferred_element_type=F32)
                acc[i] = part if c == 0 else acc[i] + part
            r0 = c * FFN_ROWS
            gate = conv(rd[0], cwg_ref, cbg_ref, r0)
            up = conv(rd[1], cwu_ref, cbu_ref, r0)
            o_ref[r0:r0 + FFN_ROWS, :] = (gate * jax.nn.sigmoid(gate) * up).astype(o_ref.dtype)

        for i, dst in enumerate(wr):
            dst[...] = acc[i]

    @pl.when(s % 2 == 0)
    def _():
        phase((ga, ua), (gb, ub))

    @pl.when(s % 2 == 1)
    def _():
        phase((gb, ub), (ga, ua))


def _ffn_up(h2, halo, w_up, conv_w, conv_b, tm, tn=256):
    t, d = h2.shape
    nj = D_FF // tn
    last = (t // tm) * nj - 1
    cb = conv_b.reshape(1, 2 * D_FF)
    mm = lambda s: jnp.minimum(s, last)
    ep = lambda s: jnp.maximum(s - 1, 0)
    return pl.pallas_call(
        functools.partial(_ffn_up_kernel, tm=tm, nj=nj, last=last),
        grid=(last + 2,),
        in_specs=[pl.BlockSpec((tm, d), lambda s: (mm(s) // nj, 0)),
                  pl.BlockSpec((1, 2 * HALO, d), lambda s: (mm(s) // nj, 0, 0)),
                  pl.BlockSpec((d, tn), lambda s: (0, mm(s) % nj)),
                  pl.BlockSpec((d, tn), lambda s: (0, mm(s) % nj + nj)),
                  pl.BlockSpec((3, tn), lambda s: (0, ep(s) % nj)),
                  pl.BlockSpec((3, tn), lambda s: (0, ep(s) % nj + nj)),
                  pl.BlockSpec((1, tn), lambda s: (0, ep(s) % nj)),
                  pl.BlockSpec((1, tn), lambda s: (0, ep(s) % nj + nj))],
        out_specs=pl.BlockSpec((tm, tn), lambda s: (ep(s) // nj, ep(s) % nj)),
        out_shape=jax.ShapeDtypeStruct((t, D_FF), BF16),
        scratch_shapes=[pltpu.VMEM((tm + 2 * HALO, d), BF16)] + [pltpu.VMEM((tm + 2 * HALO, tn), F32)] * 4,
        compiler_params=_params(1),
        name="ffn_up",
    )(h2, halo, w_up, w_up, conv_w, conv_w, cb, cb)


def _conv_halo(h2, tm, seq):
    t, d = h2.shape
    nt = t // tm
    hr = h2.reshape(nt, tm, d)
    zero = jnp.zeros((1, HALO, d), h2.dtype)
    prev = jnp.concatenate([zero, hr[:-1, tm - HALO:, :]], axis=0)
    nxt = jnp.concatenate([hr[1:, :HALO, :], zero], axis=0)
    start = (np.arange(nt) * tm) % seq == 0
    end = ((np.arange(nt) + 1) * tm) % seq == 0
    prev = jnp.where(jnp.asarray(start)[:, None, None], 0, prev)
    nxt = jnp.where(jnp.asarray(end)[:, None, None], 0, nxt)
    return jnp.concatenate([prev, nxt], axis=1)


def kernel(x, norm1_g, w_in, qn_na, kn_na, qn_dil, kn_dil, rel_bias, out_norm_g, w_out, norm2_g, w_up,
           conv_w, conv_b, w_down):
    batch, seq, d = x.shape
    t = batch * seq
    scale = HEAD_DIM ** -0.5
    ffn_tm = 1024
    xf = x.reshape(t, d)
    dil_tab = _dil_tables()
    ones = jnp.ones((D_GRP,), F32)
    for l in range(norm1_g.shape[0]):
        h = _rmsnorm(xf, norm1_g[l])
        gcol = jnp.concatenate([jnp.tile(qn_na[l] * scale, N_HEADS), jnp.tile(kn_na[l], N_HEADS), ones,
                                jnp.tile(qn_dil[l] * scale, N_HEADS), jnp.tile(kn_dil[l], N_HEADS), ones])
        qkv, qkv_d2, qkv_d3, w_up_b, w_down_b, w_out_b = _qkv_proj(
            h, w_in[l].astype(BF16), gcol.reshape(1, -1), (w_up[l], w_down[l], w_out[l]), batch, seq)
        out_na = _na_attention(qkv, _na_tables(rel_bias[l], seq // GRID_W), batch, seq)
        out_dil = _dil_attention(((qkv.reshape(batch, 1, seq, -1), 3 * N_HEADS), (qkv_d2, 0), (qkv_d3, 0)),
                                 dil_tab, batch, seq)
        mix = _group_norm(out_na, out_dil, out_norm_g[l])
        xf = _mm_res(mix, w_out_b, xf, tm=1024, tn=1024, name="out_proj")
        h2 = _rmsnorm(xf, norm2_g[l])
        act = _ffn_up(h2, _conv_halo(h2, ffn_tm, seq), w_up_b, conv_w[l], conv_b[l], tm=ffn_tm)
        xf = _mm_res(act, w_down_b, xf, tm=512, tn=512, name="down_proj")
    return xf.reshape(batch, seq, d)
```

```python
import functools

import jax
import jax.numpy as jnp
import numpy as np
from jax import lax
from jax.experimental import pallas as pl
from jax.experimental.pallas import tpu as pltpu

D_MODEL = 4096
HEAD_DIM = 128
N_HEADS = 16
D_GRP = N_HEADS * HEAD_DIM
GRID_W = 64
NA_ROWS = 8
NA_COLS = 16
DIL_PAIRS = ((128, 1), (512, 4), (2048, 16))
DIL_BLOCK = 64
DIL_HALF = 64
D_FF = 11008
EPS = 1e-6
NEG_INF = -1e30
LOG2E = 1.4426950408889634

NA_QROWS = 4
NA_KROWS = NA_QROWS + NA_ROWS - 1
NA_QB = NA_QROWS * GRID_W
NA_KB = NA_KROWS * GRID_W

VMEM_LIMIT = 56 * 1024 * 1024

F32 = jnp.float32
BF16 = jnp.bfloat16


def _params(n_axes):
    return pltpu.CompilerParams(dimension_semantics=("arbitrary",) * n_axes,
                                vmem_limit_bytes=VMEM_LIMIT)


def _rmsnorm_kernel(x_ref, g_ref, o_ref):
    x = x_ref[...]
    ms = jnp.mean(x * x, axis=-1, keepdims=True)
    o_ref[...] = (x * lax.rsqrt(ms + EPS) * g_ref[...]).astype(o_ref.dtype)


def _rmsnorm(x, g, tr=512):
    t, d = x.shape
    return pl.pallas_call(
        _rmsnorm_kernel,
        grid=(t // tr,),
        in_specs=[pl.BlockSpec((tr, d), lambda i: (i, 0)),
                  pl.BlockSpec((1, d), lambda i: (0, 0))],
        out_specs=pl.BlockSpec((tr, d), lambda i: (i, 0)),
        out_shape=jax.ShapeDtypeStruct((t, d), BF16),
        compiler_params=_params(1),
        name="rmsnorm",
    )(x, g.reshape(1, d))


def _group_norm_kernel(a_ref, b_ref, g_ref, o_ref):
    for idx, ref in enumerate((a_ref, b_ref)):
        x = ref[...]
        ms = jnp.mean(x * x, axis=-1, keepdims=True)
        lo = idx * D_GRP
        o_ref[:, lo:lo + D_GRP] = (x * lax.rsqrt(ms + EPS) * g_ref[:, lo:lo + D_GRP]).astype(o_ref.dtype)


def _group_norm(a, b, g, tr=512):
    t = a.shape[0]
    return pl.pallas_call(
        _group_norm_kernel,
        grid=(t // tr,),
        in_specs=[pl.BlockSpec((tr, D_GRP), lambda i: (i, 0)),
                  pl.BlockSpec((tr, D_GRP), lambda i: (i, 0)),
                  pl.BlockSpec((1, 2 * D_GRP), lambda i: (0, 0))],
        out_specs=pl.BlockSpec((tr, 2 * D_GRP), lambda i: (i, 0)),
        out_shape=jax.ShapeDtypeStruct((t, 2 * D_GRP), BF16),
        compiler_params=_params(1),
        name="group_norm",
    )(a, b, g.reshape(1, 2 * D_GRP))


DIL2 = DIL_PAIRS[1][1]
DIL3 = DIL_PAIRS[2][1]


QKV_SLICES = 16
CAST_BLOCKS = 128
CAST_BLOCK = ((128, 5504), (688, 512), (32, 4096))
CAST_PIECE = ((16, 5504), (688, 128), (16, 4096))


def _qkv_kernel(h_ref, w_ref, g_ref, c1_ref, c2_ref, c3_ref, nat_ref, od2_ref, od3_ref, b1_ref, b2_ref, b3_ref,
                slab, *, tm, tn, nj, nna, last):
    s = pl.program_id(0)
    j = jnp.minimum(s, last) % nj
    is_v = ((j * tn) // D_GRP) % 3 == 2
    nslab = tn // HEAD_DIM
    half = tn // 2
    kc = h_ref.shape[1] // QKV_SLICES
    cur = s % 2
    prev = 1 - cur

    @pl.when(s == 0)
    def _():
        slab[...] = jnp.zeros_like(slab)

    @pl.when(s < CAST_BLOCKS)
    def _():
        for src, dst, (br, bc), (pr, pc) in zip((c1_ref, c2_ref, c3_ref), (b1_ref, b2_ref, b3_ref), CAST_BLOCK,
                                                CAST_PIECE):
            for r in range(0, br, pr):
                for c in range(0, bc, pc):
                    dst[r:r + pr, c:c + pc] = src[r:r + pr, c:c + pc].astype(BF16)

    regroup = [(o_ref, dil, r, k) for o_ref, dil in ((od2_ref, DIL2), (od3_ref, DIL3))
               for r in range(dil) for k in range(nslab)]
    halves = [None, None]
    for c in range(QKV_SLICES):
        ksl = slice(c * kc, (c + 1) * kc)
        for i in range(2):
            part = jnp.dot(h_ref[:, ksl], w_ref[ksl, i * half:(i + 1) * half], preferred_element_type=F32)
            halves[i] = part if c == 0 else halves[i] + part
        for o_ref, dil, r, k in regroup[c * len(regroup) // QKV_SLICES:(c + 1) * len(regroup) // QKV_SLICES]:
            o_ref[0, r, :, k * HEAD_DIM:(k + 1) * HEAD_DIM] = (
                slab[prev, k, pl.ds(r, tm // dil, stride=dil), :].astype(BF16))

    def normed(k, norm):
        lanes = slice(k * HEAD_DIM, (k + 1) * HEAD_DIM)
        per_half = half // HEAD_DIM
        x = halves[k // per_half][:, (k % per_half) * HEAD_DIM:(k % per_half + 1) * HEAD_DIM]
        if not norm:
            return x
        ms = jnp.mean(x * x, axis=-1, keepdims=True)
        return x * lax.rsqrt(ms + EPS) * g_ref[:, lanes]

    def epilogue(norm, dilated):
        for k in range(nslab):
            y = normed(k, norm)
            if dilated:
                slab[cur, k, :, :] = y
            nat_ref[:, k * HEAD_DIM:(k + 1) * HEAD_DIM] = y.astype(BF16)

    for norm in (True, False):
        for dilated in (False, True):
            cond = jnp.logical_and(is_v != norm, (j >= nna) == dilated)
            pl.when(cond)(functools.partial(epilogue, norm, dilated))


def _qkv_proj(h, w, gcol, side_weights, batch, seq, tm=1024, tn=512):
    t, d = h.shape
    n = w.shape[1]
    nj = n // tn
    nna = 3 * D_GRP // tn
    tps = seq // tm
    assert seq % tm == 0 and D_GRP % tn == 0 and tm % (16 * DIL3) == 0 and tn % (2 * HEAD_DIM) == 0
    last = (t // tm) * nj - 1
    assert last + 1 >= CAST_BLOCKS
    mm = lambda s: jnp.minimum(s, last)
    ep = lambda s: jnp.maximum(s - 1, 0)
    grouped = lambda s: (ep(s) // nj // tps, 0, ep(s) // nj % tps, jnp.maximum(ep(s) % nj - nna, 0))
    ndil = 3 * D_GRP
    cast_specs = []
    for arr, (br, bc) in zip(side_weights, CAST_BLOCK):
        ncol = arr.shape[1] // bc
        assert arr.shape[0] % br == 0 and arr.shape[1] % bc == 0 and (arr.shape[0] // br) * ncol == CAST_BLOCKS
        cast_specs.append(pl.BlockSpec(
            (br, bc), lambda s, ncol=ncol: (jnp.minimum(s, CAST_BLOCKS - 1) // ncol,
                                            jnp.minimum(s, CAST_BLOCKS - 1) % ncol)))
    return pl.pallas_call(
        functools.partial(_qkv_kernel, tm=tm, tn=tn, nj=nj, nna=nna, last=last),
        grid=(last + 2,),
        in_specs=[pl.BlockSpec((tm, d), lambda s: (mm(s) // nj, 0)),
                  pl.BlockSpec((d, tn), lambda s: (0, mm(s) % nj)),
                  pl.BlockSpec((1, tn), lambda s: (0, mm(s) % nj))] + cast_specs,
        out_specs=[pl.BlockSpec((tm, tn), lambda s: (mm(s) // nj, mm(s) % nj)),
                   pl.BlockSpec((1, DIL2, tm // DIL2, tn), grouped),
                   pl.BlockSpec((1, DIL3, tm // DIL3, tn), grouped)]
        + cast_specs,
        out_shape=[jax.ShapeDtypeStruct((t, n), BF16),
                   jax.ShapeDtypeStruct((batch, DIL2, seq // DIL2, ndil), BF16),
                   jax.ShapeDtypeStruct((batch, DIL3, seq // DIL3, ndil), BF16)]
        + [jax.ShapeDtypeStruct(a.shape, BF16) for a in side_weights],
        scratch_shapes=[pltpu.VMEM((2, tn // HEAD_DIM, tm, HEAD_DIM), F32)],
        compiler_params=_params(1),
        name="qkv_proj",
    )(h, w, gcol, *side_weights)


def _na_tables(rel_bias, rows):
    nblk = rows // NA_QROWS
    qr = np.arange(NA_QROWS)[:, None]
    kr = np.arange(NA_KROWS)[None, :]
    dr_idx = np.zeros((3, NA_QROWS, NA_KROWS), np.int32)
    row_ok = np.zeros((3, NA_QROWS, NA_KROWS), bool)
    for var, j in enumerate((0, 1, nblk - 1)):
        kstart = int(np.clip(NA_QROWS * j - NA_QROWS, 0, rows - NA_KROWS))
        r = NA_QROWS * j + qr
        r0 = np.clip(r - NA_ROWS // 2, 0, rows - NA_ROWS)
        ka = kstart + kr
        row_ok[var] = (ka >= r0) & (ka < r0 + NA_ROWS)
        dr_idx[var] = np.clip(ka - r + NA_ROWS - 1, 0, 2 * NA_ROWS - 2)
    col = np.arange(GRID_W)
    c0 = np.clip(col - NA_COLS // 2, 0, GRID_W - NA_COLS)
    col_ok = (col[None, :] >= c0[:, None]) & (col[None, :] < c0[:, None] + NA_COLS)
    dc_idx = np.clip(col[None, :] - col[:, None] + NA_COLS - 1, 0, 2 * NA_COLS - 2)

    col_bias = (rel_bias * LOG2E)[:, :, dc_idx]
    bias = col_bias[:, dr_idx]
    ok = row_ok[:, :, :, None, None] & col_ok[None, None, None, :, :]
    bias = jnp.where(ok[None], bias.astype(F32), NEG_INF)
    bias = bias.transpose(0, 1, 2, 4, 3, 5)
    return bias.reshape(rel_bias.shape[0], 3, NA_QB, NA_KB)


def _na_kernel(q_ref, k_ref, v_ref, tab_ref, o_ref, *, rows):
    nblk = rows // NA_QROWS

    def body(j, carry):
        krow = jnp.clip(NA_QROWS * j - NA_QROWS, 0, rows - NA_KROWS)
        var = jnp.where(j == 0, 0, jnp.where(j == nblk - 1, 2, 1))
        qs = pl.multiple_of(j * NA_QB, NA_QB)
        ks = pl.multiple_of(krow * GRID_W, GRID_W)
        q = q_ref[pl.ds(qs, NA_QB), :]
        k = k_ref[pl.ds(ks, NA_KB), :]
        v = v_ref[pl.ds(ks, NA_KB), :]
        s = lax.dot_general(q, k, (((1,), (1,)), ((), ())), preferred_element_type=F32)
        s = s + tab_ref[0, var]
        m = jnp.max(s, axis=-1, keepdims=True)
        p = jnp.exp2(s - m)
        l = jnp.sum(p, axis=-1, keepdims=True)
        o = jnp.dot(p.astype(BF16), v, preferred_element_type=F32)
        o_ref[pl.ds(qs, NA_QB), :] = o / l
        return carry

    lax.fori_loop(0, nblk, body, 0, unroll=4)


def _na_attention(proj, tab, batch, seq):
    t = proj.shape[0]
    rows = seq // GRID_W
    qkv_spec = lambda off: pl.BlockSpec((seq, HEAD_DIM), lambda h, b: (b, off + h))
    return pl.pallas_call(
        functools.partial(_na_kernel, rows=rows),
        grid=(N_HEADS, batch),
        in_specs=[qkv_spec(0), qkv_spec(N_HEADS), qkv_spec(2 * N_HEADS),
                  pl.BlockSpec((1, 3, NA_QB, NA_KB), lambda h, b: (h, 0, 0, 0))],
        out_specs=pl.BlockSpec((seq, HEAD_DIM), lambda h, b: (b, h)),
        out_shape=jax.ShapeDtypeStruct((t, D_GRP), F32),
        compiler_params=_params(2),
        name="na_attention",
    )(proj, proj, proj, tab)


DIL_QB = 2 * DIL_BLOCK
DIL_KB = DIL_QB + 2 * DIL_HALF


def _dil_tables():
    slopes = jnp.exp2(-8.0 * jnp.arange(1, N_HEADS + 1, dtype=F32) / N_HEADS)
    i = np.arange(DIL_QB)[:, None]
    j = np.arange(DIL_KB)[None, :]
    rel = np.stack([j - off - i for off in (0, DIL_HALF, DIL_KB - DIL_QB)])
    ok = np.abs(rel) <= DIL_HALF
    tabs = []
    for _, dil in DIL_PAIRS:
        dist = jnp.asarray((dil * np.abs(rel)).astype(np.float32))
        pen = -(slopes[:, None, None, None] * dist[None]) * LOG2E
        tabs.append(jnp.where(ok[None], pen, NEG_INF))
    return jnp.stack(tabs, axis=1)


def _dil_kernel(q1, k1, v1, q2, k2, v2, q3, k3, v3, tab_ref, o_ref, op0, op1, lp0, lp1, *, seq):
    qkv = ((q1, k1, v1), (q2, k2, v2), (q3, k3, v3))
    o_res = (op0, op1)
    l_res = (lp0, lp1)
    d2, d3 = DIL_PAIRS[1][1], DIL_PAIRS[2][1]
    sub2 = seq // d2
    step = d3 // d2
    for br, (_, dil) in enumerate(DIL_PAIRS):
        sub = seq // dil
        nblk = sub // DIL_QB

        def block(idx, carry, br=br, sub=sub, nblk=nblk):
            q_ref, k_ref, v_ref = qkv[br]
            r = idx // nblk
            n = idx % nblk
            var = jnp.where(n == 0, 0, jnp.where(n == nblk - 1, 2, 1))
            q0 = pl.multiple_of(n * DIL_QB, DIL_QB)
            k0 = pl.multiple_of(jnp.clip(n * DIL_QB - DIL_HALF, 0, sub - DIL_KB), DIL_HALF)
            qs = pl.multiple_of(r * sub + n * DIL_QB, DIL_QB)
            q = q_ref[0, r, pl.ds(q0, DIL_QB), :]
            k = k_ref[0, r, pl.ds(k0, DIL_KB), :]
            v = v_ref[0, r, pl.ds(k0, DIL_KB), :]
            s = lax.dot_general(q, k, (((1,), (1,)), ((), ())), preferred_element_type=F32)
            s = s + tab_ref[0, br, var]
            m = jnp.max(s, axis=-1, keepdims=True)
            p = jnp.exp2(s - m)
            l = jnp.sum(p, axis=-1, keepdims=True)
            o = jnp.dot(p.astype(BF16), v, preferred_element_type=F32)
            o3 = o / l
            l3 = jnp.broadcast_to(m + jnp.log2(l), (DIL_QB, HEAD_DIM))
            if br < len(DIL_PAIRS) - 1:
                o_res[br][pl.ds(qs, DIL_QB), :] = o3
                l_res[br][pl.ds(qs, DIL_QB), :] = l3
            else:
                s2 = pl.ds((r % d2) * sub2 + step * q0 + r // d2, DIL_QB, stride=step)
                s1 = pl.ds(d3 * q0 + r, DIL_QB, stride=d3)
                l1, l2 = lp0[s1, :], lp1[s2, :]
                mx = jnp.maximum(jnp.maximum(l1, l2), l3)
                w1, w2, w3 = jnp.exp2(l1 - mx), jnp.exp2(l2 - mx), jnp.exp2(l3 - mx)
                num = w1 * op0[s1, :] + w2 * op1[s2, :] + w3 * o3
                o_ref[s1, :] = num / (w1 + w2 + w3)
            return carry

        lax.fori_loop(0, dil * nblk, block, 0, unroll=16)


def _dil_attention(grouped, tab, batch, seq):
    dils = tuple(d for _, d in DIL_PAIRS)
    assert dils[0] == 1 and dils[2] % dils[1] == 0 and all(w // (2 * d) == DIL_HALF for w, d in DIL_PAIRS)
    assert all((seq // d) % DIL_QB == 0 and seq // d >= DIL_KB for d in dils)
    in_specs, args = [], []
    for (arr, base), dil in zip(grouped, dils):
        for off in (base, base + N_HEADS, base + 2 * N_HEADS):
            in_specs.append(pl.BlockSpec((1, dil, seq // dil, HEAD_DIM), lambda h, b, off=off: (b, 0, 0, off + h)))
            args.append(arr)
    in_specs.append(pl.BlockSpec((1, len(DIL_PAIRS), 3, DIL_QB, DIL_KB), lambda h, b: (h, 0, 0, 0, 0)))
    return pl.pallas_call(
        functools.partial(_dil_kernel, seq=seq),
        grid=(N_HEADS, batch),
        in_specs=in_specs,
        out_specs=pl.BlockSpec((seq, HEAD_DIM), lambda h, b: (b, h)),
        out_shape=jax.ShapeDtypeStruct((batch * seq, D_GRP), F32),
        scratch_shapes=[pltpu.VMEM((seq, HEAD_DIM), F32)] * 4,
        compiler_params=_params(2),
        name="dil_attention",
    )(*args, tab)


def _mm_res_kernel(a_ref, w_ref, r_ref, o_ref):
    o_ref[...] = r_ref[...] + jnp.dot(a_ref[...], w_ref[...], preferred_element_type=F32)


def _mm_res(a, w, res, tm, tn, name):
    t, kdim = a.shape
    n = w.shape[1]
    return pl.pallas_call(
        _mm_res_kernel,
        grid=(t // tm, n // tn),
        in_specs=[pl.BlockSpec((tm, kdim), lambda i, j: (i, 0)),
                  pl.BlockSpec((kdim, tn), lambda i, j: (0, j)),
                  pl.BlockSpec((tm, tn), lambda i, j: (i, j))],
        out_specs=pl.BlockSpec((tm, tn), lambda i, j: (i, j)),
        out_shape=jax.ShapeDtypeStruct((t, n), F32),
        compiler_params=_params(2),
        name=name,
    )(a, w, res)


HALO = 16
FFN_ROWS = 64


def _ffn_up_kernel(h_ref, halo_ref, wg_ref, wu_ref, cwg_ref, cwu_ref, cbg_ref, cbu_ref, o_ref,
                   hx, ga, ua, gb, ub, *, tm, nj, last):
    s = pl.program_id(0)

    @pl.when(s == 0)
    def _():
        gb[...] = jnp.zeros_like(gb)
        ub[...] = jnp.zeros_like(ub)

    @pl.when(jnp.logical_and(s % nj == 0, s <= last))
    def _():
        hx[0:HALO, :] = halo_ref[0, 0:HALO, :]
        hx[HALO:HALO + tm, :] = h_ref[...]
        hx[HALO + tm:2 * HALO + tm, :] = halo_ref[0, HALO:2 * HALO, :]

    def phase(wr, rd):
        def conv(src, cw_ref, cb_ref, r0):
            prev = src[HALO - 1 + r0:HALO - 1 + r0 + FFN_ROWS, :]
            cur = src[HALO + r0:HALO + r0 + FFN_ROWS, :]
            nxt = src[HALO + 1 + r0:HALO + 1 + r0 + FFN_ROWS, :]
            return cw_ref[0:1, :] * prev + cw_ref[1:2, :] * cur + cw_ref[2:3, :] * nxt + cb_ref[...]

        d = hx.shape[1]
        nchunk = tm // FFN_ROWS
        kc = d // nchunk
        acc = [None, None]
        for c in range(nchunk):
            ksl = slice(c * kc, (c + 1) * kc)
            for i, w_ref in enumerate((wg_ref, wu_ref)):
                part = jnp.dot(hx[:, ksl], w_ref[ksl, :], preferred_element_type=F32)
                acc[i] = part if c == 0 else acc[i] + part
            r0 = c * FFN_ROWS
            gate = conv(rd[0], cwg_ref, cbg_ref, r0)
            up = conv(rd[1], cwu_ref, cbu_ref, r0)
            o_ref[r0:r0 + FFN_ROWS, :] = (gate * jax.nn.sigmoid(gate) * up).astype(o_ref.dtype)

        for i, dst in enumerate(wr):
            dst[...] = acc[i]

    @pl.when(s % 2 == 0)
    def _():
        phase((ga, ua), (gb, ub))

    @pl.when(s % 2 == 1)
    def _():
        phase((gb, ub), (ga, ua))


def _ffn_up(h2, halo, w_up, conv_w, conv_b, tm, tn=256):
    t, d = h2.shape
    nj = D_FF // tn
    last = (t // tm) * nj - 1
    cb = conv_b.reshape(1, 2 * D_FF)
    mm = lambda s: jnp.minimum(s, last)
    ep = lambda s: jnp.maximum(s - 1, 0)
    return pl.pallas_call(
        functools.partial(_ffn_up_kernel, tm=tm, nj=nj, last=last),
        grid=(last + 2,),
        in_specs=[pl.BlockSpec((tm, d), lambda s: (mm(s) // nj, 0)),
                  pl.BlockSpec((1, 2 * HALO, d), lambda s: (mm(s) // nj, 0, 0)),
                  pl.BlockSpec((d, tn), lambda s: (0, mm(s) % nj)),
                  pl.BlockSpec((d, tn), lambda s: (0, mm(s) % nj + nj)),
                  pl.BlockSpec((3, tn), lambda s: (0, ep(s) % nj)),
                  pl.BlockSpec((3, tn), lambda s: (0, ep(s) % nj + nj)),
                  pl.BlockSpec((1, tn), lambda s: (0, ep(s) % nj)),
                  pl.BlockSpec((1, tn), lambda s: (0, ep(s) % nj + nj))],
        out_specs=pl.BlockSpec((tm, tn), lambda s: (ep(s) // nj, ep(s) % nj)),
        out_shape=jax.ShapeDtypeStruct((t, D_FF), BF16),
        scratch_shapes=[pltpu.VMEM((tm + 2 * HALO, d), BF16)] + [pltpu.VMEM((tm + 2 * HALO, tn), F32)] * 4,
        compiler_params=_params(1),
        name="ffn_up",
    )(h2, halo, w_up, w_up, conv_w, conv_w, cb, cb)


def _conv_halo(h2, tm, seq):
    t, d = h2.shape
    nt = t // tm
    hr = h2.reshape(nt, tm, d)
    zero = jnp.zeros((1, HALO, d), h2.dtype)
    prev = jnp.concatenate([zero, hr[:-1, tm - HALO:, :]], axis=0)
    nxt = jnp.concatenate([hr[1:, :HALO, :], zero], axis=0)
    start = (np.arange(nt) * tm) % seq == 0
    end = ((np.arange(nt) + 1) * tm) % seq == 0
    prev = jnp.where(jnp.asarray(start)[:, None, None], 0, prev)
    nxt = jnp.where(jnp.asarray(end)[:, None, None], 0, nxt)
    return jnp.concatenate([prev, nxt], axis=1)


def kernel(x, norm1_g, w_in, qn_na, kn_na, qn_dil, kn_dil, rel_bias, out_norm_g, w_out, norm2_g, w_up,
           conv_w, conv_b, w_down):
    batch, seq, d = x.shape
    t = batch * seq
    scale = HEAD_DIM ** -0.5 * LOG2E
    ffn_tm = 1024
    xf = x.reshape(t, d)
    dil_tab = _dil_tables()
    ones = jnp.ones((D_GRP,), F32)
    for l in range(norm1_g.shape[0]):
        h = _rmsnorm(xf, norm1_g[l])
        gcol = jnp.concatenate([jnp.tile(qn_na[l] * scale, N_HEADS), jnp.tile(kn_na[l], N_HEADS), ones,
                                jnp.tile(qn_dil[l] * scale, N_HEADS), jnp.tile(kn_dil[l], N_HEADS), ones])
        qkv, qkv_d2, qkv_d3, w_up_b, w_down_b, w_out_b = _qkv_proj(
            h, w_in[l].astype(BF16), gcol.reshape(1, -1), (w_up[l], w_down[l], w_out[l]), batch, seq)
        out_na = _na_attention(qkv, _na_tables(rel_bias[l], seq // GRID_W), batch, seq)
        out_dil = _dil_attention(((qkv.reshape(batch, 1, seq, -1), 3 * N_HEADS), (qkv_d2, 0), (qkv_d3, 0)),
                                 dil_tab, batch, seq)
        mix = _group_norm(out_na, out_dil, out_norm_g[l])
        xf = _mm_res(mix, w_out_b, xf, tm=1024, tn=1024, name="out_proj")
        h2 = _rmsnorm(xf, norm2_g[l])
        act = _ffn_up(h2, _conv_halo(h2, ffn_tm, seq), w_up_b, conv_w[l], conv_b[l], tm=ffn_tm)
        xf = _mm_res(act, w_down_b, xf, tm=512, tn=512, name="down_proj")
    return xf.reshape(batch, seq, d)
```

```python
import functools

import jax
import jax.numpy as jnp
import numpy as np
from jax import lax
from jax.experimental import pallas as pl
from jax.experimental.pallas import tpu as pltpu

D_MODEL = 4096
HEAD_DIM = 128
N_HEADS = 16
D_GRP = N_HEADS * HEAD_DIM
GRID_W = 64
NA_ROWS = 8
NA_COLS = 16
DIL_PAIRS = ((128, 1), (512, 4), (2048, 16))
DIL_BLOCK = 64
DIL_HALF = 64
D_FF = 11008
EPS = 1e-6
NEG_INF = -1e30
LOG2E = 1.4426950408889634

NA_QROWS = 4
NA_KROWS = NA_QROWS + NA_ROWS - 1
NA_QB = NA_QROWS * GRID_W
NA_KB = NA_KROWS * GRID_W

VMEM_LIMIT = 56 * 1024 * 1024

F32 = jnp.float32
BF16 = jnp.bfloat16


def _params(n_axes):
    return pltpu.CompilerParams(dimension_semantics=("arbitrary",) * n_axes,
                                vmem_limit_bytes=VMEM_LIMIT)


def _rmsnorm_kernel(x_ref, g_ref, o_ref):
    x = x_ref[...]
    ms = jnp.mean(x * x, axis=-1, keepdims=True)
    o_ref[...] = (x * lax.rsqrt(ms + EPS) * g_ref[...]).astype(o_ref.dtype)


def _rmsnorm(x, g, tr=512):
    t, d = x.shape
    return pl.pallas_call(
        _rmsnorm_kernel,
        grid=(t // tr,),
        in_specs=[pl.BlockSpec((tr, d), lambda i: (i, 0)),
                  pl.BlockSpec((1, d), lambda i: (0, 0))],
        out_specs=pl.BlockSpec((tr, d), lambda i: (i, 0)),
        out_shape=jax.ShapeDtypeStruct((t, d), BF16),
        compiler_params=_params(1),
        name="rmsnorm",
    )(x, g.reshape(1, d))


def _group_norm_kernel(a_ref, b_ref, g_ref, o_ref):
    for idx, ref in enumerate((a_ref, b_ref)):
        x = ref[...]
        ms = jnp.mean(x * x, axis=-1, keepdims=True)
        lo = idx * D_GRP
        o_ref[:, lo:lo + D_GRP] = (x * lax.rsqrt(ms + EPS) * g_ref[:, lo:lo + D_GRP]).astype(o_ref.dtype)


def _group_norm(a, b, g, tr=512):
    t = a.shape[0]
    return pl.pallas_call(
        _group_norm_kernel,
        grid=(t // tr,),
        in_specs=[pl.BlockSpec((tr, D_GRP), lambda i: (i, 0)),
                  pl.BlockSpec((tr, D_GRP), lambda i: (i, 0)),
                  pl.BlockSpec((1, 2 * D_GRP), lambda i: (0, 0))],
        out_specs=pl.BlockSpec((tr, 2 * D_GRP), lambda i: (i, 0)),
        out_shape=jax.ShapeDtypeStruct((t, 2 * D_GRP), BF16),
        compiler_params=_params(1),
        name="group_norm",
    )(a, b, g.reshape(1, 2 * D_GRP))


DIL2 = DIL_PAIRS[1][1]
DIL3 = DIL_PAIRS[2][1]


QKV_SLICES = 16
CAST_BLOCKS = 128
CAST_BLOCK = ((128, 5504), (688, 512), (32, 4096))
CAST_PIECE = ((16, 5504), (688, 128), (16, 4096))


def _qkv_kernel(h_ref, w_ref, g_ref, c1_ref, c2_ref, c3_ref, nat_ref, od2_ref, od3_ref, b1_ref, b2_ref, b3_ref,
                slab, *, tm, tn, nj, nna, last):
    s = pl.program_id(0)
    j = jnp.minimum(s, last) % nj
    is_v = ((j * tn) // D_GRP) % 3 == 2
    nslab = tn // HEAD_DIM
    half = tn // 2
    kc = h_ref.shape[1] // QKV_SLICES
    cur = s % 2
    prev = 1 - cur

    @pl.when(s == 0)
    def _():
        slab[...] = jnp.zeros_like(slab)

    @pl.when(s < CAST_BLOCKS)
    def _():
        for src, dst, (br, bc), (pr, pc) in zip((c1_ref, c2_ref, c3_ref), (b1_ref, b2_ref, b3_ref), CAST_BLOCK,
                                                CAST_PIECE):
            for r in range(0, br, pr):
                for c in range(0, bc, pc):
                    dst[r:r + pr, c:c + pc] = src[r:r + pr, c:c + pc].astype(BF16)

    regroup = [(o_ref, dil, r, k) for o_ref, dil in ((od2_ref, DIL2), (od3_ref, DIL3))
               for r in range(dil) for k in range(nslab)]
    halves = [None, None]
    for c in range(QKV_SLICES):
        ksl = slice(c * kc, (c + 1) * kc)
        for i in range(2):
            part = jnp.dot(h_ref[:, ksl], w_ref[ksl, i * half:(i + 1) * half], preferred_element_type=F32)
            halves[i] = part if c == 0 else halves[i] + part
        for o_ref, dil, r, k in regroup[c * len(regroup) // QKV_SLICES:(c + 1) * len(regroup) // QKV_SLICES]:
            o_ref[0, r, :, k * HEAD_DIM:(k + 1) * HEAD_DIM] = (
                slab[prev, k, pl.ds(r, tm // dil, stride=dil), :].astype(BF16))

    def normed(k, norm):
        lanes = slice(k * HEAD_DIM, (k + 1) * HEAD_DIM)
        per_half = half // HEAD_DIM
        x = halves[k // per_half][:, (k % per_half) * HEAD_DIM:(k % per_half + 1) * HEAD_DIM]
        if not norm:
            return x
        ms = jnp.mean(x * x, axis=-1, keepdims=True)
        return x * lax.rsqrt(ms + EPS) * g_ref[:, lanes]

    def epilogue(norm, dilated):
        for k in range(nslab):
            y = normed(k, norm)
            if dilated:
                slab[cur, k, :, :] = y
            nat_ref[:, k * HEAD_DIM:(k + 1) * HEAD_DIM] = y.astype(BF16)

    for norm in (True, False):
        for dilated in (False, True):
            cond = jnp.logical_and(is_v != norm, (j >= nna) == dilated)
            pl.when(cond)(functools.partial(epilogue, norm, dilated))


def _qkv_proj(h, w, gcol, side_weights, batch, seq, tm=1024, tn=512):
    t, d = h.shape
    n = w.shape[1]
    nj = n // tn
    nna = 3 * D_GRP // tn
    tps = seq // tm
    assert seq % tm == 0 and D_GRP % tn == 0 and tm % (16 * DIL3) == 0 and tn % (2 * HEAD_DIM) == 0
    last = (t // tm) * nj - 1
    assert last + 1 >= CAST_BLOCKS
    mm = lambda s: jnp.minimum(s, last)
    ep = lambda s: jnp.maximum(s - 1, 0)
    grouped = lambda s: (ep(s) // nj // tps, 0, ep(s) // nj % tps, jnp.maximum(ep(s) % nj - nna, 0))
    ndil = 3 * D_GRP
    cast_specs = []
    for arr, (br, bc) in zip(side_weights, CAST_BLOCK):
        ncol = arr.shape[1] // bc
        assert arr.shape[0] % br == 0 and arr.shape[1] % bc == 0 and (arr.shape[0] // br) * ncol == CAST_BLOCKS
        cast_specs.append(pl.BlockSpec(
            (br, bc), lambda s, ncol=ncol: (jnp.minimum(s, CAST_BLOCKS - 1) // ncol,
                                            jnp.minimum(s, CAST_BLOCKS - 1) % ncol)))
    return pl.pallas_call(
        functools.partial(_qkv_kernel, tm=tm, tn=tn, nj=nj, nna=nna, last=last),
        grid=(last + 2,),
        in_specs=[pl.BlockSpec((tm, d), lambda s: (mm(s) // nj, 0)),
                  pl.BlockSpec((d, tn), lambda s: (0, mm(s) % nj)),
                  pl.BlockSpec((1, tn), lambda s: (0, mm(s) % nj))] + cast_specs,
        out_specs=[pl.BlockSpec((tm, tn), lambda s: (mm(s) // nj, mm(s) % nj)),
                   pl.BlockSpec((1, DIL2, tm // DIL2, tn), grouped),
                   pl.BlockSpec((1, DIL3, tm // DIL3, tn), grouped)]
        + cast_specs,
        out_shape=[jax.ShapeDtypeStruct((t, n), BF16),
                   jax.ShapeDtypeStruct((batch, DIL2, seq // DIL2, ndil), BF16),
                   jax.ShapeDtypeStruct((batch, DIL3, seq // DIL3, ndil), BF16)]
        + [jax.ShapeDtypeStruct(a.shape, BF16) for a in side_weights],
        scratch_shapes=[pltpu.VMEM((2, tn // HEAD_DIM, tm, HEAD_DIM), F32)],
        compiler_params=_params(1),
        name="qkv_proj",
    )(h, w, gcol, *side_weights)


def _na_tables(rel_bias, rows):
    nblk = rows // NA_QROWS
    qr = np.arange(NA_QROWS)[:, None]
    kr = np.arange(NA_KROWS)[None, :]
    dr_idx = np.zeros((3, NA_QROWS, NA_KROWS), np.int32)
    row_ok = np.zeros((3, NA_QROWS, NA_KROWS), bool)
    for var, j in enumerate((0, 1, nblk - 1)):
        kstart = int(np.clip(NA_QROWS * j - NA_QROWS, 0, rows - NA_KROWS))
        r = NA_QROWS * j + qr
        r0 = np.clip(r - NA_ROWS // 2, 0, rows - NA_ROWS)
        ka = kstart + kr
        row_ok[var] = (ka >= r0) & (ka < r0 + NA_ROWS)
        dr_idx[var] = np.clip(ka - r + NA_ROWS - 1, 0, 2 * NA_ROWS - 2)
    col = np.arange(GRID_W)
    c0 = np.clip(col - NA_COLS // 2, 0, GRID_W - NA_COLS)
    col_ok = (col[None, :] >= c0[:, None]) & (col[None, :] < c0[:, None] + NA_COLS)
    dc_idx = np.clip(col[None, :] - col[:, None] + NA_COLS - 1, 0, 2 * NA_COLS - 2)

    col_bias = (rel_bias * LOG2E)[:, :, dc_idx]
    bias = col_bias[:, dr_idx]
    ok = row_ok[:, :, :, None, None] & col_ok[None, None, None, :, :]
    bias = jnp.where(ok[None], bias.astype(F32), NEG_INF)
    bias = bias.transpose(0, 1, 2, 4, 3, 5)
    return bias.reshape(rel_bias.shape[0], 3, NA_QB, NA_KB)


def _na_kernel(q_ref, k_ref, v_ref, tab_ref, o_ref, *, rows):
    nblk = rows // NA_QROWS

    def body(j, carry):
        krow = jnp.clip(NA_QROWS * j - NA_QROWS, 0, rows - NA_KROWS)
        var = jnp.where(j == 0, 0, jnp.where(j == nblk - 1, 2, 1))
        qs = pl.multiple_of(j * NA_QB, NA_QB)
        ks = pl.multiple_of(krow * GRID_W, GRID_W)
        q = q_ref[pl.ds(qs, NA_QB), :]
        k = k_ref[pl.ds(ks, NA_KB), :]
        v = v_ref[pl.ds(ks, NA_KB), :]
        s = lax.dot_general(q, k, (((1,), (1,)), ((), ())), preferred_element_type=F32)
        s = s + tab_ref[0, var]
        m = jnp.max(s, axis=-1, keepdims=True)
        p = jnp.exp2(s - m)
        l = jnp.sum(p, axis=-1, keepdims=True)
        o = jnp.dot(p.astype(BF16), v, preferred_element_type=F32)
        o_ref[pl.ds(qs, NA_QB), :] = o / l
        return carry

    lax.fori_loop(0, nblk, body, 0, unroll=8)


def _na_attention(proj, tab, batch, seq):
    t = proj.shape[0]
    rows = seq // GRID_W
    qkv_spec = lambda off: pl.BlockSpec((seq, HEAD_DIM), lambda h, b: (b, off + h))
    return pl.pallas_call(
        functools.partial(_na_kernel, rows=rows),
        grid=(N_HEADS, batch),
        in_specs=[qkv_spec(0), qkv_spec(N_HEADS), qkv_spec(2 * N_HEADS),
                  pl.BlockSpec((1, 3, NA_QB, NA_KB), lambda h, b: (h, 0, 0, 0))],
        out_specs=pl.BlockSpec((seq, HEAD_DIM), lambda h, b: (b, h)),
        out_shape=jax.ShapeDtypeStruct((t, D_GRP), F32),
        compiler_params=_params(2),
        name="na_attention",
    )(proj, proj, proj, tab)


DIL_QB = 2 * DIL_BLOCK
DIL_KB = DIL_QB + 2 * DIL_HALF


def _dil_tables():
    slopes = jnp.exp2(-8.0 * jnp.arange(1, N_HEADS + 1, dtype=F32) / N_HEADS)
    i = np.arange(DIL_QB)[:, None]
    j = np.arange(DIL_KB)[None, :]
    rel = np.stack([j - off - i for off in (0, DIL_HALF, DIL_KB - DIL_QB)])
    ok = np.abs(rel) <= DIL_HALF
    tabs = []
    for _, dil in DIL_PAIRS:
        dist = jnp.asarray((dil * np.abs(rel)).astype(np.float32))
        pen = -(slopes[:, None, None, None] * dist[None]) * LOG2E
        tabs.append(jnp.where(ok[None], pen, NEG_INF))
    return jnp.stack(tabs, axis=1)


def _dil_kernel(q1, k1, v1, q2, k2, v2, q3, k3, v3, tab_ref, o_ref, op0, op1, lp0, lp1, *, seq):
    qkv = ((q1, k1, v1), (q2, k2, v2), (q3, k3, v3))
    o_res = (op0, op1)
    l_res = (lp0, lp1)
    d2, d3 = DIL_PAIRS[1][1], DIL_PAIRS[2][1]
    sub2 = seq // d2
    step = d3 // d2
    for br, (_, dil) in enumerate(DIL_PAIRS):
        sub = seq // dil
        nblk = sub // DIL_QB

        def block(idx, carry, br=br, sub=sub, nblk=nblk):
            q_ref, k_ref, v_ref = qkv[br]
            r = idx // nblk
            n = idx % nblk
            var = jnp.where(n == 0, 0, jnp.where(n == nblk - 1, 2, 1))
            q0 = pl.multiple_of(n * DIL_QB, DIL_QB)
            k0 = pl.multiple_of(jnp.clip(n * DIL_QB - DIL_HALF, 0, sub - DIL_KB), DIL_HALF)
            qs = pl.multiple_of(r * sub + n * DIL_QB, DIL_QB)
            q = q_ref[0, r, pl.ds(q0, DIL_QB), :]
            k = k_ref[0, r, pl.ds(k0, DIL_KB), :]
            v = v_ref[0, r, pl.ds(k0, DIL_KB), :]
            s = lax.dot_general(q, k, (((1,), (1,)), ((), ())), preferred_element_type=F32)
            s = s + tab_ref[0, br, var]
            m = jnp.max(s, axis=-1, keepdims=True)
            p = jnp.exp2(s - m)
            l = jnp.sum(p, axis=-1, keepdims=True)
            o = jnp.dot(p.astype(BF16), v, preferred_element_type=F32)
            o3 = o / l
            l3 = jnp.broadcast_to(m + jnp.log2(l), (DIL_QB, HEAD_DIM))
            if br < len(DIL_PAIRS) - 1:
                o_res[br][pl.ds(qs, DIL_QB), :] = o3
                l_res[br][pl.ds(qs, DIL_QB), :] = l3
            else:
                s2 = pl.ds((r % d2) * sub2 + step * q0 + r // d2, DIL_QB, stride=step)
                s1 = pl.ds(d3 * q0 + r, DIL_QB, stride=d3)
                l1, l2 = lp0[s1, :], lp1[s2, :]
                mx = jnp.maximum(jnp.maximum(l1, l2), l3)
                w1, w2, w3 = jnp.exp2(l1 - mx), jnp.exp2(l2 - mx), jnp.exp2(l3 - mx)
                num = w1 * op0[s1, :] + w2 * op1[s2, :] + w3 * o3
                o_ref[s1, :] = num / (w1 + w2 + w3)
            return carry

        lax.fori_loop(0, dil * nblk, block, 0, unroll=16)


def _dil_attention(grouped, tab, batch, seq):
    dils = tuple(d for _, d in DIL_PAIRS)
    assert dils[0] == 1 and dils[2] % dils[1] == 0 and all(w // (2 * d) == DIL_HALF for w, d in DIL_PAIRS)
    assert all((seq // d) % DIL_QB == 0 and seq // d >= DIL_KB for d in dils)
    in_specs, args = [], []
    for (arr, base), dil in zip(grouped, dils):
        for off in (base, base + N_HEADS, base + 2 * N_HEADS):
            in_specs.append(pl.BlockSpec((1, dil, seq // dil, HEAD_DIM), lambda h, b, off=off: (b, 0, 0, off + h)))
            args.append(arr)
    in_specs.append(pl.BlockSpec((1, len(DIL_PAIRS), 3, DIL_QB, DIL_KB), lambda h, b: (h, 0, 0, 0, 0)))
    return pl.pallas_call(
        functools.partial(_dil_kernel, seq=seq),
        grid=(N_HEADS, batch),
        in_specs=in_specs,
        out_specs=pl.BlockSpec((seq, HEAD_DIM), lambda h, b: (b, h)),
        out_shape=jax.ShapeDtypeStruct((batch * seq, D_GRP), F32),
        scratch_shapes=[pltpu.VMEM((seq, HEAD_DIM), F32)] * 4,
        compiler_params=_params(2),
        name="dil_attention",
    )(*args, tab)


def _mm_res_kernel(a_ref, w_ref, r_ref, o_ref):
    o_ref[...] = r_ref[...] + jnp.dot(a_ref[...], w_ref[...], preferred_element_type=F32)


def _mm_res(a, w, res, tm, tn, name):
    t, kdim = a.shape
    n = w.shape[1]
    return pl.pallas_call(
        _mm_res_kernel,
        grid=(t // tm, n // tn),
        in_specs=[pl.BlockSpec((tm, kdim), lambda i, j: (i, 0)),
                  pl.BlockSpec((kdim, tn), lambda i, j: (0, j)),
                  pl.BlockSpec((tm, tn), lambda i, j: (i, j))],
        out_specs=pl.BlockSpec((tm, tn), lambda i, j: (i, j)),
        out_shape=jax.ShapeDtypeStruct((t, n), F32),
        compiler_params=_params(2),
        name=name,
    )(a, w, res)


HALO = 8
FFN_ROWS = 64


def _ffn_up_kernel(h_ref, halo_ref, wg_ref, wu_ref, cwg_ref, cwu_ref, cbg_ref, cbu_ref, o_ref,
                   hx, ga, ua, gb, ub, *, tm, nj, last):
    s = pl.program_id(0)

    @pl.when(s == 0)
    def _():
        gb[...] = jnp.zeros_like(gb)
        ub[...] = jnp.zeros_like(ub)

    @pl.when(jnp.logical_and(s % nj == 0, s <= last))
    def _():
        hx[0:tm, :] = h_ref[...]
        hx[tm:tm + 2 * HALO, :] = halo_ref[0]

    def phase(wr, rd):
        row = lax.broadcasted_iota(jnp.int32, (FFN_ROWS, 1), 0)

        def conv(src, cw_ref, cb_ref, r0):
            cur = src[r0:r0 + FFN_ROWS, :]
            if r0 == 0:
                prev = jnp.where(row == 0, src[tm + HALO - 1:tm + HALO, :], pltpu.roll(cur, 1, axis=0))
            else:
                prev = src[r0 - 1:r0 - 1 + FFN_ROWS, :]
            if r0 + FFN_ROWS == tm:
                nxt = jnp.where(row == FFN_ROWS - 1, src[tm + HALO:tm + HALO + 1, :],
                                pltpu.roll(cur, FFN_ROWS - 1, axis=0))
            else:
                nxt = src[r0 + 1:r0 + 1 + FFN_ROWS, :]
            return cw_ref[0:1, :] * prev + cw_ref[1:2, :] * cur + cw_ref[2:3, :] * nxt + cb_ref[...]

        d = hx.shape[1]
        nchunk = tm // FFN_ROWS
        kc = d // nchunk
        acc = [None, None]
        for c in range(nchunk):
            ksl = slice(c * kc, (c + 1) * kc)
            for i, w_ref in enumerate((wg_ref, wu_ref)):
                part = jnp.dot(hx[:, ksl], w_ref[ksl, :], preferred_element_type=F32)
                acc[i] = part if c == 0 else acc[i] + part
            r0 = c * FFN_ROWS
            gate = conv(rd[0], cwg_ref, cbg_ref, r0)
            up = conv(rd[1], cwu_ref, cbu_ref, r0)
            o_ref[r0:r0 + FFN_ROWS, :] = (gate * jax.nn.sigmoid(gate) * up).astype(o_ref.dtype)

        for i, dst in enumerate(wr):
            dst[...] = acc[i]

    @pl.when(s % 2 == 0)
    def _():
        phase((ga, ua), (gb, ub))

    @pl.when(s % 2 == 1)
    def _():
        phase((gb, ub), (ga, ua))


def _ffn_up(h2, halo, w_up, conv_w, conv_b, tm, tn=256):
    t, d = h2.shape
    nj = D_FF // tn
    last = (t // tm) * nj - 1
    cb = conv_b.reshape(1, 2 * D_FF)
    mm = lambda s: jnp.minimum(s, last)
    ep = lambda s: jnp.maximum(s - 1, 0)
    return pl.pallas_call(
        functools.partial(_ffn_up_kernel, tm=tm, nj=nj, last=last),
        grid=(last + 2,),
        in_specs=[pl.BlockSpec((tm, d), lambda s: (mm(s) // nj, 0)),
                  pl.BlockSpec((1, 2 * HALO, d), lambda s: (mm(s) // nj, 0, 0)),
                  pl.BlockSpec((d, tn), lambda s: (0, mm(s) % nj)),
                  pl.BlockSpec((d, tn), lambda s: (0, mm(s) % nj + nj)),
                  pl.BlockSpec((3, tn), lambda s: (0, ep(s) % nj)),
                  pl.BlockSpec((3, tn), lambda s: (0, ep(s) % nj + nj)),
                  pl.BlockSpec((1, tn), lambda s: (0, ep(s) % nj)),
                  pl.BlockSpec((1, tn), lambda s: (0, ep(s) % nj + nj))],
        out_specs=pl.BlockSpec((tm, tn), lambda s: (ep(s) // nj, ep(s) % nj)),
        out_shape=jax.ShapeDtypeStruct((t, D_FF), BF16),
        scratch_shapes=[pltpu.VMEM((tm + 2 * HALO, d), BF16)] + [pltpu.VMEM((tm + 2 * HALO, tn), F32)] * 4,
        compiler_params=_params(1),
        name="ffn_up",
    )(h2, halo, w_up, w_up, conv_w, conv_w, cb, cb)


def _conv_halo(h2, tm, seq):
    t, d = h2.shape
    nt = t // tm
    hr = h2.reshape(nt, tm, d)
    zero = jnp.zeros((1, HALO, d), h2.dtype)
    prev = jnp.concatenate([zero, hr[:-1, tm - HALO:, :]], axis=0)
    nxt = jnp.concatenate([hr[1:, :HALO, :], zero], axis=0)
    start = (np.arange(nt) * tm) % seq == 0
    end = ((np.arange(nt) + 1) * tm) % seq == 0
    prev = jnp.where(jnp.asarray(start)[:, None, None], 0, prev)
    nxt = jnp.where(jnp.asarray(end)[:, None, None], 0, nxt)
    return jnp.concatenate([prev, nxt], axis=1)


def kernel(x, norm1_g, w_in, qn_na, kn_na, qn_dil, kn_dil, rel_bias, out_norm_g, w_out, norm2_g, w_up,
           conv_w, conv_b, w_down):
    batch, seq, d = x.shape
    t = batch * seq
    scale = HEAD_DIM ** -0.5 * LOG2E
    ffn_tm = 1024
    xf = x.reshape(t, d)
    dil_tab = _dil_tables()
    ones = jnp.ones((D_GRP,), F32)
    for l in range(norm1_g.shape[0]):
        h = _rmsnorm(xf, norm1_g[l])
        gcol = jnp.concatenate([jnp.tile(qn_na[l] * scale, N_HEADS), jnp.tile(kn_na[l], N_HEADS), ones,
                                jnp.tile(qn_dil[l] * scale, N_HEADS), jnp.tile(kn_dil[l], N_HEADS), ones])
        qkv, qkv_d2, qkv_d3, w_up_b, w_down_b, w_out_b = _qkv_proj(
            h, w_in[l].astype(BF16), gcol.reshape(1, -1), (w_up[l], w_down[l], w_out[l]), batch, seq)
        out_na = _na_attention(qkv, _na_tables(rel_bias[l], seq // GRID_W), batch, seq)
        out_dil = _dil_attention(((qkv.reshape(batch, 1, seq, -1), 3 * N_HEADS), (qkv_d2, 0), (qkv_d3, 0)),
                                 dil_tab, batch, seq)
        mix = _group_norm(out_na, out_dil, out_norm_g[l])
        xf = _mm_res(mix, w_out_b, xf, tm=1024, tn=1024, name="out_proj")
        h2 = _rmsnorm(xf, norm2_g[l])
        act = _ffn_up(h2, _conv_halo(h2, ffn_tm, seq), w_up_b, conv_w[l], conv_b[l], tm=ffn_tm)
        xf = _mm_res(act, w_down_b, xf, tm=512, tn=512, name="down_proj")
    return xf.reshape(batch, seq, d)
```
